```python
import jax, jax.numpy as jnp
from jax import lax
import numpy as np

D_MODEL = 1024
BATCH = 8
SEQ = 8192
DEPTH = 4

CHUNK = 64
POOL_WIDTH = D_MODEL // 4
POOL_WINDOWS = (2, 4, 8, 16)
POOL_GROUPS = len(POOL_WINDOWS)
POOL_GROUP_DIM = POOL_WIDTH // POOL_GROUPS
CONV_WIDTH = D_MODEL // 4
CONV_KERNEL = 31
ATTN_HEADS = 8
HEAD_DIM = 64
ATTN_WIDTH = ATTN_HEADS * HEAD_DIM
N_BRANCH = 3
Q_BLOCK = 128
D_FF = 2816
N_EXPERTS = 8
TOP_K = 2
D_FF_EXPERT = 3584
EPS = 1e-6
N_DENSE = (DEPTH + 1) // 2
N_MOE = DEPTH // 2

OFF_POOL = 0
OFF_CONV = OFF_POOL + POOL_WIDTH
OFF_Q = OFF_CONV + 2 * CONV_WIDTH
OFF_K = OFF_Q + ATTN_WIDTH
OFF_V = OFF_K + ATTN_WIDTH
OFF_F = OFF_V + ATTN_WIDTH
OFF_G = OFF_F + ATTN_HEADS
IN_COLS = OFF_G + N_BRANCH * D_MODEL

kernel_name = "hybrid_pool_conv_fox_moe_trunk"


def rms_norm(x, g):
    xf = x.astype(jnp.float32)
    y = xf * lax.rsqrt(jnp.mean(xf * xf, axis=-1, keepdims=True) + EPS)
    return (y * g.astype(jnp.float32)).astype(x.dtype)


def layer_norm(x, g, b):
    xf = x.astype(jnp.float32)
    mu = jnp.mean(xf, axis=-1, keepdims=True)
    var = jnp.mean(jnp.square(xf - mu), axis=-1, keepdims=True)
    y = (xf - mu) * lax.rsqrt(var + EPS)
    return (y * g.astype(jnp.float32) + b.astype(jnp.float32)).astype(x.dtype)


def pool_mixer(p, w_group, scale):
    S = p.shape[1]
    pf = p.astype(jnp.float32)
    cs = jnp.cumsum(pf, axis=1)
    t = jnp.arange(1, S + 1, dtype=jnp.float32)
    outs = []
    for g, w in enumerate(POOL_WINDOWS):
        lo, hi = g * POOL_GROUP_DIM, (g + 1) * POOL_GROUP_DIM
        csg = cs[..., lo:hi]
        lagged = jnp.pad(csg, ((0, 0), (w, 0), (0, 0)))[:, :S]
        count = jnp.minimum(t, float(w))[None, :, None]
        pooled = (csg - lagged) / count - pf[..., lo:hi]
        outs.append(jnp.einsum('bsc,cd->bsd', pooled.astype(p.dtype), w_group[g]))
    return jnp.concatenate(outs, axis=-1) * scale


def conv_module(u, conv_w, conv_b, ln_g, ln_b):
    a, gate = jnp.split(u, 2, axis=-1)
    z = a * jax.nn.sigmoid(gate)
    z = lax.conv_general_dilated(
        z, conv_w[:, None, :].astype(z.dtype), window_strides=(1,),
        padding=[(CONV_KERNEL - 1, 0)], dimension_numbers=('NWC', 'WIO', 'NWC'),
        feature_group_count=CONV_WIDTH) + conv_b
    z = layer_norm(z, ln_g, ln_b)
    return jax.nn.silu(z)


def forgetting_attention(q, k, v, f_logit, q_g, k_g):
    B, S, H, Dh = q.shape
    q = rms_norm(q, q_g)
    k = rms_norm(k, k_g)
    logf = jax.nn.log_sigmoid(f_logit.astype(jnp.float32))
    c = jnp.transpose(jnp.cumsum(logf, axis=1), (0, 2, 1))
    nb = S // Q_BLOCK
    qb = q.reshape(B, nb, Q_BLOCK, H, Dh).transpose(1, 0, 3, 2, 4)
    cqb = c.reshape(B, H, nb, Q_BLOCK).transpose(2, 0, 1, 3)
    kpos = jnp.arange(S)
    scale = HEAD_DIM ** -0.5

    def block(args):
        qi, cqi, i = args
        s = jnp.einsum('bhqd,bshd->bhqs', qi, k, preferred_element_type=jnp.float32) * scale
        s = s + (cqi[..., None] - c[:, :, None, :])
        qpos = i * Q_BLOCK + jnp.arange(Q_BLOCK)
        s = jnp.where(kpos[None, :] <= qpos[:, None], s, -jnp.inf)
        prob = jax.nn.softmax(s, axis=-1).astype(v.dtype)
        return jnp.einsum('bhqs,bshd->bhqd', prob, v)

    o = lax.map(block, (qb, cqb, jnp.arange(nb)))
    return o.transpose(1, 0, 3, 2, 4).reshape(B, S, H * Dh)


def swiglu(h, w1, w3, w2):
    a = jnp.einsum('bsd,df->bsf', h, w1)
    b = jnp.einsum('bsd,df->bsf', h, w3)
    return jnp.einsum('bsf,fd->bsd', jax.nn.silu(a) * b, w2)


def moe_ffn(h, w_router, w1, w3, w2):
    logits = jnp.einsum('bsd,de->bse', h, w_router, preferred_element_type=jnp.float32)
    top_val, top_idx = lax.top_k(logits, TOP_K)
    top_w = jax.nn.softmax(top_val, axis=-1)
    gates = jnp.sum(jax.nn.one_hot(top_idx, N_EXPERTS, dtype=jnp.float32) * top_w[..., None], axis=-2)
    gates = gates.astype(h.dtype)
    out = jnp.zeros_like(h)
    for e in range(N_EXPERTS):
        out = out + gates[..., e:e + 1] * swiglu(h, w1[e], w3[e], w2[e])
    return out


def setup_inputs(seed: int = 0) -> dict:
    key = jax.random.key(seed)
    ks = jax.random.split(key, 32)
    L, D = DEPTH, D_MODEL
    nrm = lambda k, shape, fan_in: jax.random.normal(k, shape, jnp.float32) * (fan_in ** -0.5)
    gain = lambda k, shape: 1.0 + 0.05 * jax.random.normal(k, shape, jnp.float32)
    small = lambda k, shape: 0.02 * jax.random.normal(k, shape, jnp.float32)
    return {
        "x": jax.random.normal(ks[0], (BATCH, SEQ, D), jnp.float32),
        "g_mix": gain(ks[1], (L, D)),
        "w_in": nrm(ks[2], (L, D, IN_COLS), D),
        "b_gate": small(ks[3], (L, N_BRANCH * D)),
        "b_forget": jax.random.uniform(ks[4], (L, ATTN_HEADS), jnp.float32, 1.0, 6.0),
        "pool_w": nrm(ks[5], (L, POOL_GROUPS, POOL_GROUP_DIM, POOL_GROUP_DIM), POOL_GROUP_DIM),
        "pool_scale": gain(ks[6], (L, POOL_WIDTH)),
        "conv_w": nrm(ks[7], (L, CONV_KERNEL, CONV_WIDTH), CONV_KERNEL),
        "conv_b": small(ks[8], (L, CONV_WIDTH)),
        "conv_ln_g": gain(ks[9], (L, CONV_WIDTH)),
        "conv_ln_b": small(ks[10], (L, CONV_WIDTH)),
        "q_norm_g": gain(ks[11], (L, HEAD_DIM)),
        "k_norm_g": gain(ks[12], (L, HEAD_DIM)),
        "w_br_pool": nrm(ks[13], (L, POOL_WIDTH, D), POOL_WIDTH),
        "w_br_conv": nrm(ks[14], (L, CONV_WIDTH, D), CONV_WIDTH),
        "w_br_attn": nrm(ks[15], (L, ATTN_WIDTH, D), ATTN_WIDTH),
        "w_out": nrm(ks[16], (L, D, D), D),
        "g_ffn": gain(ks[17], (L, D)),
        "ffn_w1": nrm(ks[18], (N_DENSE, D, D_FF), D),
        "ffn_w3": nrm(ks[19], (N_DENSE, D, D_FF), D),
        "ffn_w2": nrm(ks[20], (N_DENSE, D_FF, D), D_FF),
        "router_w": nrm(ks[21], (N_MOE, D, N_EXPERTS), D),
        "exp_w1": nrm(ks[22], (N_MOE, N_EXPERTS, D, D_FF_EXPERT), D),
        "exp_w3": nrm(ks[23], (N_MOE, N_EXPERTS, D, D_FF_EXPERT), D),
        "exp_w2": nrm(ks[24], (N_MOE, N_EXPERTS, D_FF_EXPERT, D), D_FF_EXPERT),
    }


def reference(x, g_mix, w_in, b_gate, b_forget, pool_w, pool_scale, conv_w, conv_b,
              conv_ln_g, conv_ln_b, q_norm_g, k_norm_g, w_br_pool, w_br_conv, w_br_attn,
              w_out, g_ffn, ffn_w1, ffn_w3, ffn_w2, router_w, exp_w1, exp_w3, exp_w2):
    B, S, D = x.shape
    for l in range(DEPTH):
        h = rms_norm(x, g_mix[l])
        u = jnp.einsum('bsd,dc->bsc', h, w_in[l])
        y_pool = pool_mixer(u[..., OFF_POOL:OFF_CONV], pool_w[l], pool_scale[l])
        y_conv = conv_module(u[..., OFF_CONV:OFF_Q], conv_w[l], conv_b[l], conv_ln_g[l], conv_ln_b[l])
        q = u[..., OFF_Q:OFF_K].reshape(B, S, ATTN_HEADS, HEAD_DIM)
        k = u[..., OFF_K:OFF_V].reshape(B, S, ATTN_HEADS, HEAD_DIM)
        v = u[..., OFF_V:OFF_F].reshape(B, S, ATTN_HEADS, HEAD_DIM)
        f_logit = u[..., OFF_F:OFF_G] + b_forget[l]
        y_attn = forgetting_attention(q, k, v, f_logit, q_norm_g[l], k_norm_g[l])
        gates = jax.nn.sigmoid(u[..., OFF_G:] + b_gate[l]).reshape(B, S, N_BRANCH, D)
        merged = (gates[:, :, 0] * jnp.einsum('bsc,cd->bsd', y_pool, w_br_pool[l])
                  + gates[:, :, 1] * jnp.einsum('bsc,cd->bsd', y_conv, w_br_conv[l])
                  + gates[:, :, 2] * jnp.einsum('bsc,cd->bsd', y_attn, w_br_attn[l]))
        x = x + jnp.einsum('bsd,de->bse', merged, w_out[l])
        h = rms_norm(x, g_ffn[l])
        if l % 2 == 0:
            i = l // 2
            x = x + swiglu(h, ffn_w1[i], ffn_w3[i], ffn_w2[i])
        else:
            i = l // 2
            x = x + moe_ffn(h, router_w[i], exp_w1[i], exp_w3[i], exp_w2[i])
    return x
```

```python
import functools

import jax
import jax.numpy as jnp
from jax import lax
from jax.experimental import pallas as pl
from jax.experimental.pallas import tpu as pltpu

F32 = jnp.float32
BF16 = jnp.bfloat16

EPS = 1e-6
POOL_WINDOWS = (2, 4, 8, 16)
POOL_WIDTH = 256
POOL_GROUP_DIM = 64
CONV_WIDTH = 256
CONV_KERNEL = 31
ATTN_HEADS = 8
HEAD_DIM = 64
ATTN_WIDTH = ATTN_HEADS * HEAD_DIM
N_BRANCH = 3
N_EXPERTS = 8
LANES = 128
SUBLANES = 8
HALO = 32
VMEM_LIMIT = 56 * 1024 * 1024


def _params(*sem):
    return pltpu.CompilerParams(dimension_semantics=sem, vmem_limit_bytes=VMEM_LIMIT)


def _const_spec(shape):
    nd = len(shape)
    return pl.BlockSpec(shape, lambda *_: (0,) * nd)


def _inproj_kernel(x_ref, g_ref, wloc_ref, wqkv_ref, wf_ref, wg_ref, bg_ref, bf_ref, qg_ref, kg_ref, bd_ref,
                   pool_ref, z_ref, q_ref, k_ref, v_ref, c_ref, gates_ref, carry_ref):
    tm = x_ref.shape[1]

    @pl.when(pl.program_id(1) == 0)
    def _():
        carry_ref[...] = jnp.zeros_like(carry_ref)

    x = x_ref[0]
    ms = jnp.mean(x * x, axis=-1, keepdims=True)
    h = (x * lax.rsqrt(ms + EPS) * g_ref[...]).astype(BF16)

    loc = jnp.dot(h, wloc_ref[...], preferred_element_type=F32)
    pool_ref[0] = loc[:, :POOL_WIDTH]
    z_ref[0] = loc[:, POOL_WIDTH:POOL_WIDTH + CONV_WIDTH] * jax.nn.sigmoid(loc[:, POOL_WIDTH + CONV_WIDTH:])

    qkv = jnp.dot(h, wqkv_ref[...], preferred_element_type=F32)
    q = qkv[:, :ATTN_WIDTH]
    k = qkv[:, ATTN_WIDTH:2 * ATTN_WIDTH]
    qss = jnp.dot((q * q).astype(BF16), bd_ref[...], preferred_element_type=F32)
    kss = jnp.dot((k * k).astype(BF16), bd_ref[...], preferred_element_type=F32)
    q_ref[0] = (q * lax.rsqrt(qss + EPS) * qg_ref[...]).astype(BF16)
    k_ref[0] = (k * lax.rsqrt(kss + EPS) * kg_ref[...]).astype(BF16)
    v_ref[0] = qkv[:, 2 * ATTN_WIDTH:].astype(BF16)

    fl = jnp.dot(h, wf_ref[...], preferred_element_type=F32) + bf_ref[...]
    cs = jnp.minimum(fl, 0.0) - jnp.log1p(jnp.exp(-jnp.abs(fl)))
    row = lax.broadcasted_iota(jnp.int32, cs.shape, 0)
    shift = 1
    while shift < tm:
        cs = cs + jnp.where(row >= shift, pltpu.roll(cs, shift, axis=0), 0.0)
        shift *= 2
    cs = cs + carry_ref[0:1, :]
    c_ref[0] = cs
    carry_ref[0:1, :] = cs[tm - 1:tm, :]

    gl = jnp.dot(h, wg_ref[...], preferred_element_type=F32) + bg_ref[...]
    gates_ref[0] = jax.nn.sigmoid(gl).astype(BF16)


def _inproj(x, g, wloc, wqkv, wf, wg, bg, bf, qg, kg, bd, *, tm):
    b, s, d = x.shape
    grid = (b, s // tm)
    row = lambda w: pl.BlockSpec((1, tm, w), lambda bi, si: (bi, si, 0))
    out_shape = (
        jax.ShapeDtypeStruct((b, s, POOL_WIDTH), F32),
        jax.ShapeDtypeStruct((b, s, CONV_WIDTH), F32),
        jax.ShapeDtypeStruct((b, s, ATTN_WIDTH), BF16),
        jax.ShapeDtypeStruct((b, s, ATTN_WIDTH), BF16),
        jax.ShapeDtypeStruct((b, s, ATTN_WIDTH), BF16),
        jax.ShapeDtypeStruct((b, s, LANES), F32),
        jax.ShapeDtypeStruct((b, s, N_BRANCH * d), BF16),
    )
    return pl.pallas_call(
        _inproj_kernel,
        grid=grid,
        in_specs=[row(d)] + [_const_spec(a.shape) for a in (g, wloc, wqkv, wf, wg, bg, bf, qg, kg, bd)],
        out_specs=(row(POOL_WIDTH), row(CONV_WIDTH), row(ATTN_WIDTH), row(ATTN_WIDTH), row(ATTN_WIDTH),
                   row(LANES), row(N_BRANCH * d)),
        out_shape=out_shape,
        scratch_shapes=[pltpu.VMEM((SUBLANES, LANES), F32)],
        compiler_params=_params("arbitrary", "arbitrary"),
        name="inproj",
    )(x, g, wloc, wqkv, wf, wg, bg, bf, qg, kg, bd)


def _local_kernel(p_ref, z_ref, wpd_ref, pscale_ref, wlane_ref, cw_ref, cb_ref, lng_ref, lnb_ref,
                  yp_ref, yc_ref, pbuf, zbuf):
    tm = p_ref.shape[1]
    si = pl.program_id(1)

    @pl.when(si == 0)
    def _():
        pbuf[0:HALO, :] = jnp.zeros((HALO, POOL_WIDTH), F32)
        zbuf[0:HALO, :] = jnp.zeros((HALO, CONV_WIDTH), F32)

    p = p_ref[0]
    pbuf[HALO:HALO + tm, :] = p
    zbuf[HALO:HALO + tm, :] = z_ref[0]

    wlane = wlane_ref[...]
    acc = p
    pooled_sum = jnp.zeros_like(p)
    for j in range(1, max(POOL_WINDOWS)):
        acc = acc + pbuf[HALO - j:HALO - j + tm, :]
        if (j + 1) in POOL_WINDOWS:
            pooled_sum = jnp.where(wlane == float(j + 1), acc, pooled_sum)
    t1 = (lax.broadcasted_iota(jnp.int32, (tm, 1), 0) + si * tm + 1).astype(F32)
    count = jnp.minimum(t1, wlane)
    pooled = pooled_sum / count - p
    yp = jnp.dot(pooled.astype(BF16), wpd_ref[...], preferred_element_type=F32) * pscale_ref[...]
    yp_ref[0] = yp.astype(BF16)

    conv = jnp.zeros((tm, CONV_WIDTH), F32) + cb_ref[...]
    for kk in range(CONV_KERNEL):
        off = HALO - (CONV_KERNEL - 1) + kk
        conv = conv + zbuf[off:off + tm, :] * cw_ref[kk:kk + 1, :]
    mu = jnp.mean(conv, axis=-1, keepdims=True)
    cen = conv - mu
    var = jnp.mean(cen * cen, axis=-1, keepdims=True)
    y = cen * lax.rsqrt(var + EPS) * lng_ref[...] + lnb_ref[...]
    yc_ref[0] = (y * jax.nn.sigmoid(y)).astype(BF16)

    pbuf[0:HALO, :] = pbuf[tm:tm + HALO, :]
    zbuf[0:HALO, :] = zbuf[tm:tm + HALO, :]


def _local(p, z, wpd, pscale, wlane, cw, cb, lng, lnb, *, tm):
    b, s, _ = p.shape
    row = lambda w: pl.BlockSpec((1, tm, w), lambda bi, si: (bi, si, 0))
    return pl.pallas_call(
        _local_kernel,
        grid=(b, s // tm),
        in_specs=[row(POOL_WIDTH), row(CONV_WIDTH)]
        + [_const_spec(a.shape) for a in (wpd, pscale, wlane, cw, cb, lng, lnb)],
        out_specs=(row(POOL_WIDTH), row(CONV_WIDTH)),
        out_shape=(jax.ShapeDtypeStruct((b, s, POOL_WIDTH), BF16), jax.ShapeDtypeStruct((b, s, CONV_WIDTH), BF16)),
        scratch_shapes=[pltpu.VMEM((HALO + tm, POOL_WIDTH), F32), pltpu.VMEM((HALO + tm, CONV_WIDTH), F32)],
        compiler_params=_params("arbitrary", "arbitrary"),
        name="local_mixers",
    )(p, z, wpd, pscale, wlane, cw, cb, lng, lnb)


def _attn_kernel(q_ref, k_ref, v_ref, cq_ref, ck_ref, o_ref, m_ref, l_ref, acc_ref, *, tk):
    tq = q_ref.shape[1]
    hp = pl.program_id(1)
    qi = pl.program_id(2)
    q2 = q_ref[0]
    lane = lax.broadcasted_iota(jnp.int32, (tq, LANES), 1)
    cblk = cq_ref[0]
    first = lane < HEAD_DIM
    qh = (jnp.where(first, q2, jnp.zeros_like(q2)), jnp.where(first, jnp.zeros_like(q2), q2))
    cqh = tuple(jnp.sum(jnp.where(lane == 2 * hp + i, cblk, 0.0), axis=-1, keepdims=True) for i in range(2))

    m_ref[...] = jnp.full(m_ref.shape, -jnp.inf, F32)
    l_ref[...] = jnp.zeros(l_ref.shape, F32)
    acc_ref[...] = jnp.zeros(acc_ref.shape, F32)

    def step(j, masked):
        koff = pl.multiple_of(j * tk, tk)
        k2 = k_ref[0, pl.ds(koff, tk), :]
        v2 = v_ref[0, pl.ds(koff, tk), :]
        for i in range(2):
            s = lax.dot_general(qh[i], k2, (((1,), (1,)), ((), ())), preferred_element_type=F32)
            ck = ck_ref[0, 0, pl.ds(i, 1), pl.ds(koff, tk)]
            s = s + (cqh[i] - ck)
            if masked:
                qpos = lax.broadcasted_iota(jnp.int32, (tq, tk), 0) + qi * tq
                kpos = lax.broadcasted_iota(jnp.int32, (tq, tk), 1) + koff
                s = jnp.where(kpos <= qpos, s, -jnp.inf)
            m_old = m_ref[i]
            m_new = jnp.maximum(m_old, jnp.max(s, axis=-1, keepdims=True))
            alpha = jnp.exp(m_old - m_new)
            p = jnp.exp(s - m_new)
            l_ref[i] = alpha * l_ref[i] + jnp.sum(p, axis=-1, keepdims=True)
            acc_ref[i] = alpha * acc_ref[i] + jnp.dot(p.astype(BF16), v2, preferred_element_type=F32)
            m_ref[i] = m_new

    n_full = qi * (tq // tk)

    def body(j, carry):
        step(j, False)
        return carry

    lax.fori_loop(0, n_full, body, 0)
    for dgl in range(tq // tk):
        step(n_full + dgl, True)

    o = jnp.where(first, acc_ref[0] / l_ref[0], acc_ref[1] / l_ref[1])
    o_ref[0] = o.astype(BF16)


def _attention(q, k, v, c, ct, *, tq, tk):
    b, s, _ = q.shape
    npair = ATTN_HEADS // 2
    return pl.pallas_call(
        functools.partial(_attn_kernel, tk=tk),
        grid=(b, npair, s // tq),
        in_specs=[
            pl.BlockSpec((1, tq, LANES), lambda bi, hp, qi: (bi, qi, hp)),
            pl.BlockSpec((1, s, LANES), lambda bi, hp, qi: (bi, 0, hp)),
            pl.BlockSpec((1, s, LANES), lambda bi, hp, qi: (bi, 0, hp)),
            pl.BlockSpec((1, tq, LANES), lambda bi, hp, qi: (bi, qi, 0)),
            pl.BlockSpec((1, 1, 2, s), lambda bi, hp, qi: (bi, hp, 0, 0)),
        ],
        out_specs=pl.BlockSpec((1, tq, LANES), lambda bi, hp, qi: (bi, qi, hp)),
        out_shape=jax.ShapeDtypeStruct((b, s, ATTN_WIDTH), BF16),
        scratch_shapes=[pltpu.VMEM((2, tq, 1), F32), pltpu.VMEM((2, tq, 1), F32), pltpu.VMEM((2, tq, LANES), F32)],
        compiler_params=_params("arbitrary", "arbitrary", "arbitrary"),
        name="fox_attention",
    )(q, k, v, c, ct)


def _merge_core(x_ref, yp_ref, yc_ref, ya_ref, gates_ref, wp_ref, wc_ref, wa_ref, wo_ref, gf_ref):
    d = x_ref.shape[1]
    gates = gates_ref[...]
    merged = (gates[:, :d].astype(F32) * jnp.dot(yp_ref[...], wp_ref[...], preferred_element_type=F32)
              + gates[:, d:2 * d].astype(F32) * jnp.dot(yc_ref[...], wc_ref[...], preferred_element_type=F32)
              + gates[:, 2 * d:].astype(F32) * jnp.dot(ya_ref[...], wa_ref[...], preferred_element_type=F32))
    xn = x_ref[...] + jnp.dot(merged.astype(BF16), wo_ref[...], preferred_element_type=F32)
    ms = jnp.mean(xn * xn, axis=-1, keepdims=True)
    h2 = xn * lax.rsqrt(ms + EPS) * gf_ref[...]
    return xn, h2


def _merge_dense_kernel(x_ref, yp_ref, yc_ref, ya_ref, gates_ref, wp_ref, wc_ref, wa_ref, wo_ref, gf_ref,
                        xn_ref, h2_ref):
    xn, h2 = _merge_core(x_ref, yp_ref, yc_ref, ya_ref, gates_ref, wp_ref, wc_ref, wa_ref, wo_ref, gf_ref)
    xn_ref[...] = xn
    h2_ref[...] = h2.astype(BF16)


def _merge_moe_kernel(x_ref, yp_ref, yc_ref, ya_ref, gates_ref, wp_ref, wc_ref, wa_ref, wo_ref, gf_ref,
                      wr_ref, tri_ref, xn_ref, h2t_ref, route_ref, counts_ref, carry_ref):
    tm, d = x_ref.shape

    @pl.when(pl.program_id(0) == 0)
    def _():
        carry_ref[...] = jnp.zeros_like(carry_ref)

    xn, h2 = _merge_core(x_ref, yp_ref, yc_ref, ya_ref, gates_ref, wp_ref, wc_ref, wa_ref, wo_ref, gf_ref)
    xn_ref[...] = xn
    for c in range(d // LANES):
        h2t_ref[pl.ds(c, tm, stride=SUBLANES), :] = h2[:, c * LANES:(c + 1) * LANES]

    logits = jnp.dot(h2, wr_ref[...], preferred_element_type=F32, precision=lax.Precision.HIGHEST)
    lane = lax.broadcasted_iota(jnp.int32, (tm, LANES), 1)
    lanef = lane.astype(F32)
    lg = jnp.where(lane < N_EXPERTS, logits, -jnp.inf)
    m1 = jnp.max(lg, axis=-1, keepdims=True)
    i1 = jnp.min(jnp.where(lg == m1, lanef, float(LANES)), axis=-1, keepdims=True)
    lg2 = jnp.where(lanef == i1, -jnp.inf, lg)
    m2 = jnp.max(lg2, axis=-1, keepdims=True)
    i2 = jnp.min(jnp.where(lg2 == m2, lanef, float(LANES)), axis=-1, keepdims=True)
    e2 = jnp.exp(m2 - m1)
    w1 = 1.0 / (1.0 + e2)
    w2 = e2 / (1.0 + e2)

    is1 = lanef == i1
    is2 = lanef == i2
    sel = jnp.where(is1 | is2, 1.0, 0.0)
    rank = jnp.dot(tri_ref[...], sel.astype(BF16), preferred_element_type=F32) + carry_ref[0:1, :]
    r1 = jnp.sum(jnp.where(is1, rank, 0.0), axis=-1, keepdims=True)
    r2 = jnp.sum(jnp.where(is2, rank, 0.0), axis=-1, keepdims=True)
    total = rank[tm - 1:tm, :] + sel[tm - 1:tm, :]
    carry_ref[0:1, :] = total
    counts_ref[...] = jnp.broadcast_to(total, counts_ref.shape)

    route = jnp.zeros((tm, LANES), F32)
    for li, val in enumerate((i1, i2, w1, w2, r1, r2)):
        route = jnp.where(lane == li, val, route)
    route_ref[...] = route


def _merge(x2, yp, yc, ya, gates, wp, wc, wa, wo, gf, wr=None, tri=None, *, tm):
    n, d = x2.shape
    row = lambda w: pl.BlockSpec((tm, w), lambda i: (i, 0))
    ins = [x2, yp, yc, ya, gates, wp, wc, wa, wo, gf]
    in_specs = [row(d), row(POOL_WIDTH), row(CONV_WIDTH), row(ATTN_WIDTH), row(N_BRANCH * d)] + [
        _const_spec(a.shape) for a in (wp, wc, wa, wo, gf)]
    if wr is None:
        return pl.pallas_call(
            _merge_dense_kernel,
            grid=(n // tm,),
            in_specs=in_specs,
            out_specs=(row(d), row(d)),
            out_shape=(jax.ShapeDtypeStruct((n, d), F32), jax.ShapeDtypeStruct((n, d), BF16)),
            compiler_params=_params("arbitrary"),
            name="merge_dense",
        )(*ins)
    return pl.pallas_call(
        _merge_moe_kernel,
        grid=(n // tm,),
        in_specs=in_specs + [_const_spec(wr.shape), _const_spec(tri.shape)],
        out_specs=(row(d), pl.BlockSpec((tm * SUBLANES, LANES), lambda i: (i, 0)), row(LANES),
                   _const_spec((SUBLANES, LANES))),
        out_shape=(jax.ShapeDtypeStruct((n, d), F32), jax.ShapeDtypeStruct((n * SUBLANES, LANES), F32),
                   jax.ShapeDtypeStruct((n, LANES), F32), jax.ShapeDtypeStruct((SUBLANES, LANES), F32)),
        scratch_shapes=[pltpu.VMEM((SUBLANES, LANES), F32)],
        compiler_params=_params("arbitrary"),
        name="merge_moe",
    )(*ins, wr, tri)


def _swiglu_chunk(xb, w1_ref, w3_ref, w2_ref):
    a = jnp.dot(xb, w1_ref[0], preferred_element_type=F32)
    b = jnp.dot(xb, w3_ref[0], preferred_element_type=F32)
    hmid = (a * jax.nn.sigmoid(a) * b).astype(BF16)
    return jnp.dot(hmid, w2_ref[0], preferred_element_type=F32)


def _ffn_dense_kernel(x_ref, h_ref, w1_ref, w3_ref, w2_ref, o_ref, acc_ref):
    f = pl.program_id(1)

    @pl.when(f == 0)
    def _():
        acc_ref[...] = x_ref[...]

    acc_ref[...] += _swiglu_chunk(h_ref[...], w1_ref, w3_ref, w2_ref)

    @pl.when(f == pl.num_programs(1) - 1)
    def _():
        o_ref[...] = acc_ref[...]


def _ffn_dense(x2, h2, w1, w3, w2, *, tm, fc):
    n, d = x2.shape
    ff = w1.shape[2]
    row = lambda: pl.BlockSpec((tm, d), lambda i, f: (i, 0))
    return pl.pallas_call(
        _ffn_dense_kernel,
        grid=(n // tm, ff // fc),
        in_specs=[row(), row(),
                  pl.BlockSpec((1, d, fc), lambda i, f: (0, 0, f)),
                  pl.BlockSpec((1, d, fc), lambda i, f: (0, 0, f)),
                  pl.BlockSpec((1, fc, d), lambda i, f: (0, f, 0))],
        out_specs=row(),
        out_shape=jax.ShapeDtypeStruct((n, d), F32),
        scratch_shapes=[pltpu.VMEM((tm, d), F32)],
        compiler_params=_params("arbitrary", "arbitrary"),
        name="ffn_dense",
    )(x2, h2, w1, w3, w2)


def _ffn_grouped_kernel(te_ref, nused_ref, xs_ref, w1_ref, w3_ref, w2_ref, y_ref, xb_ref, acc_ref):
    t, d = xb_ref.shape
    i = pl.program_id(0)
    f = pl.program_id(1)
    live = i < nused_ref[0]

    @pl.when(f == 0)
    def _():
        for c in range(d // LANES):
            xb_ref[:, c * LANES:(c + 1) * LANES] = xs_ref[pl.ds(c, t, stride=SUBLANES), :].astype(BF16)
        acc_ref[...] = jnp.zeros_like(acc_ref)

    @pl.when(live)
    def _():
        acc_ref[...] += _swiglu_chunk(xb_ref[...], w1_ref, w3_ref, w2_ref)

    @pl.when(f == pl.num_programs(1) - 1)
    def _():
        for c in range(d // LANES):
            y_ref[pl.ds(c, t, stride=SUBLANES), :] = acc_ref[:, c * LANES:(c + 1) * LANES]


def _ffn_grouped(tile_expert, n_used, xs, w1, w3, w2, *, t, fc):
    rows, _ = xs.shape
    _, d, ff = w1.shape
    n_tiles = rows // (t * SUBLANES)
    blk = pl.BlockSpec((t * SUBLANES, LANES), lambda i, f, te, nu: (i, 0))
    grid_spec = pltpu.PrefetchScalarGridSpec(
        num_scalar_prefetch=2,
        grid=(n_tiles, ff // fc),
        in_specs=[blk,
                  pl.BlockSpec((1, d, fc), lambda i, f, te, nu: (te[i], 0, f)),
                  pl.BlockSpec((1, d, fc), lambda i, f, te, nu: (te[i], 0, f)),
                  pl.BlockSpec((1, fc, d), lambda i, f, te, nu: (te[i], f, 0))],
        out_specs=blk,
        scratch_shapes=[pltpu.VMEM((t, d), BF16), pltpu.VMEM((t, d), F32)],
    )
    return pl.pallas_call(
        _ffn_grouped_kernel,
        grid_spec=grid_spec,
        out_shape=jax.ShapeDtypeStruct(xs.shape, F32),
        compiler_params=_params("arbitrary", "arbitrary"),
        name="ffn_grouped",
    )(tile_expert, n_used, xs, w1, w3, w2)


def _dispatch_kernel(pos_ref, h_hbm, xs_in, xs_hbm, sem):
    del xs_in
    td = pos_ref.shape[2] // 2
    base = pl.program_id(0) * td

    def row_copy(tok, slot):
        return pltpu.make_async_copy(h_hbm.at[pl.ds(tok * SUBLANES, SUBLANES)],
                                     xs_hbm.at[pl.ds(slot * SUBLANES, SUBLANES)], sem)

    def issue(tt, carry):
        for kk in range(2):
            row_copy(base + tt, pos_ref[0, 0, 2 * tt + kk]).start()
        return carry

    lax.fori_loop(0, td, issue, 0)

    def drain(tt, carry):
        for kk in range(2):
            row_copy(base + tt, pos_ref[0, 0, 2 * tt + kk]).wait()
        return carry

    lax.fori_loop(0, td, drain, 0)


def _dispatch(pos2, h2t, n_slots, *, td):
    n = h2t.shape[0] // SUBLANES
    xs0 = jnp.zeros((n_slots * SUBLANES, LANES), F32)
    return pl.pallas_call(
        _dispatch_kernel,
        grid=(n // td,),
        in_specs=[pl.BlockSpec((1, 1, 2 * td), lambda i: (i, 0, 0), memory_space=pltpu.SMEM),
                  pl.BlockSpec(memory_space=pl.ANY),
                  pl.BlockSpec(memory_space=pl.ANY)],
        out_specs=pl.BlockSpec(memory_space=pl.ANY),
        out_shape=jax.ShapeDtypeStruct(xs0.shape, F32),
        scratch_shapes=[pltpu.SemaphoreType.DMA(())],
        input_output_aliases={2: 0},
        compiler_params=_params("arbitrary"),
        name="moe_dispatch",
    )(pos2, h2t, xs0)


def _combine_kernel(pos_ref, x_ref, route_ref, y_hbm, o_ref, buf, sem):
    tc, d = x_ref.shape

    def row_copy(tt, kk):
        slot = pos_ref[0, 0, 2 * tt + kk]
        return pltpu.make_async_copy(y_hbm.at[pl.ds(slot * SUBLANES, SUBLANES)],
                                     buf.at[kk, pl.ds(tt * SUBLANES, SUBLANES)], sem)

    def issue(tt, carry):
        for kk in range(2):
            row_copy(tt, kk).start()
        return carry

    lax.fori_loop(0, tc, issue, 0)

    def drain(tt, carry):
        for kk in range(2):
            row_copy(tt, kk).wait()
        return carry

    lax.fori_loop(0, tc, drain, 0)

    route = route_ref[...]
    w1 = route[:, 2:3]
    w2 = route[:, 3:4]
    for c in range(d // LANES):
        y1 = buf[0, pl.ds(c, tc, stride=SUBLANES), :]
        y2 = buf[1, pl.ds(c, tc, stride=SUBLANES), :]
        o_ref[:, c * LANES:(c + 1) * LANES] = x_ref[:, c * LANES:(c + 1) * LANES] + (w1 * y1 + w2 * y2)


def _combine(pos2, x2, route, y, *, tc):
    n, d = x2.shape
    return pl.pallas_call(
        _combine_kernel,
        grid=(n // tc,),
        in_specs=[pl.BlockSpec((1, 1, 2 * tc), lambda i: (i, 0, 0), memory_space=pltpu.SMEM),
                  pl.BlockSpec((tc, d), lambda i: (i, 0)),
                  pl.BlockSpec((tc, LANES), lambda i: (i, 0)),
                  pl.BlockSpec(memory_space=pl.ANY)],
        out_specs=pl.BlockSpec((tc, d), lambda i: (i, 0)),
        out_shape=jax.ShapeDtypeStruct((n, d), F32),
        scratch_shapes=[pltpu.VMEM((2, tc * SUBLANES, LANES), F32), pltpu.SemaphoreType.DMA(())],
        compiler_params=_params("arbitrary"),
        name="moe_combine",
    )(pos2, x2, route, y)


def _tiles(b, s):
    n = b * s
    pick = lambda full, unit: full if unit % full == 0 else unit
    return dict(
        tm_in=pick(512, s), tm_loc=pick(512, s), tq=pick(512, s), tk=pick(512, s), tm_merge=pick(512, n),
        tm_ffn=pick(1024, n), fc_dense=1408, t_moe=pick(1024, n) if n >= 8192 else 256, fc_moe=512,
        td=pick(512, n), tc=pick(256, n),
    )


def _block_diag(blocks):
    g, r, c = blocks.shape
    eye = jnp.eye(g, dtype=blocks.dtype)
    return (eye[:, None, :, None] * blocks[:, :, None, :]).reshape(g * r, g * c)


def _forward(x, g_mix, w_in, b_gate, b_forget, pool_w, pool_scale, conv_w, conv_b, conv_ln_g, conv_ln_b,
             q_norm_g, k_norm_g, w_br_pool, w_br_conv, w_br_attn, w_out, g_ffn, ffn_w1, ffn_w3, ffn_w2,
             router_w, exp_w1, exp_w3, exp_w2, tiles):
    b, s, d = x.shape
    n = b * s
    depth = w_in.shape[0]
    t = tiles

    off_conv = POOL_WIDTH
    off_q = off_conv + 2 * CONV_WIDTH
    off_f = off_q + 3 * ATTN_WIDTH
    off_g = off_f + ATTN_HEADS

    wlane = jnp.repeat(jnp.asarray(POOL_WINDOWS, F32), POOL_GROUP_DIM)[None, :]
    bd = _block_diag(jnp.full((ATTN_HEADS, HEAD_DIM, HEAD_DIM), 1.0 / HEAD_DIM, F32)).astype(BF16)
    tri = (lax.broadcasted_iota(jnp.int32, (t["tm_merge"], t["tm_merge"]), 1)
           < lax.broadcasted_iota(jnp.int32, (t["tm_merge"], t["tm_merge"]), 0)).astype(BF16)
    ffn_w1b, ffn_w3b, ffn_w2b = ffn_w1.astype(BF16), ffn_w3.astype(BF16), ffn_w2.astype(BF16)
    exp_w1b, exp_w3b, exp_w2b = exp_w1.astype(BF16), exp_w3.astype(BF16), exp_w2.astype(BF16)

    t_moe = t["t_moe"]
    n_tiles = (2 * n) // t_moe + N_EXPERTS
    n_slots = n_tiles * t_moe

    for l in range(depth):
        wl = w_in[l]
        wloc = wl[:, :off_q].astype(BF16)
        wqkv = wl[:, off_q:off_f].astype(BF16)
        wf = jnp.pad(wl[:, off_f:off_g], ((0, 0), (0, LANES - ATTN_HEADS))).astype(BF16)
        wg = wl[:, off_g:].astype(BF16)
        bf = jnp.pad(b_forget[l], (0, LANES - ATTN_HEADS))[None, :]
        qg = jnp.tile(q_norm_g[l], ATTN_HEADS)[None, :] * (HEAD_DIM ** -0.5)
        kg = jnp.tile(k_norm_g[l], ATTN_HEADS)[None, :]
        pool_in, z, q, k, v, c, gates = _inproj(
            x, g_mix[l][None, :], wloc, wqkv, wf, wg, b_gate[l][None, :], bf, qg, kg, bd, tm=t["tm_in"])

        yp, yc = _local(pool_in, z, _block_diag(pool_w[l]).astype(BF16), pool_scale[l][None, :], wlane,
                        conv_w[l], conv_b[l][None, :], conv_ln_g[l][None, :], conv_ln_b[l][None, :],
                        tm=t["tm_loc"])

        ct = jnp.transpose(c[:, :, :ATTN_HEADS], (0, 2, 1)).reshape(b, ATTN_HEADS // 2, 2, s)
        ya = _attention(q, k, v, c, ct, tq=t["tq"], tk=t["tk"])

        merge_args = (x.reshape(n, d), yp.reshape(n, -1), yc.reshape(n, -1), ya.reshape(n, -1),
                      gates.reshape(n, -1), w_br_pool[l].astype(BF16), w_br_conv[l].astype(BF16),
                      w_br_attn[l].astype(BF16), w_out[l].astype(BF16), g_ffn[l][None, :])
        i = l // 2
        if l % 2 == 0:
            xn, h2 = _merge(*merge_args, tm=t["tm_merge"])
            x2 = _ffn_dense(xn, h2, ffn_w1b[i:i + 1], ffn_w3b[i:i + 1], ffn_w2b[i:i + 1],
                            tm=t["tm_ffn"], fc=t["fc_dense"])
        else:
            wr = jnp.pad(router_w[i], ((0, 0), (0, LANES - N_EXPERTS)))
            xn, h2t, route, counts = _merge(*merge_args, wr, tri, tm=t["tm_merge"])
            cnt = counts[0, :N_EXPERTS].astype(jnp.int32)
            ntile_e = (cnt + t_moe - 1) // t_moe
            ends = jnp.cumsum(ntile_e) * t_moe
            starts = ends - ntile_e * t_moe
            idx = route[:, 0:2].astype(jnp.int32)
            rank = route[:, 4:6].astype(jnp.int32)
            pos = starts[idx] + rank
            tile_first = jnp.arange(n_tiles, dtype=jnp.int32) * t_moe
            tile_expert = jnp.minimum(jnp.sum(tile_first[:, None] >= ends[None, :], axis=1),
                                      N_EXPERTS - 1).astype(jnp.int32)
            n_used = (ends[-1:] // t_moe).astype(jnp.int32)
            xs = _dispatch(pos.reshape(n // t["td"], 1, 2 * t["td"]), h2t, n_slots, td=t["td"])
            y = _ffn_grouped(tile_expert, n_used, xs, exp_w1b[i], exp_w3b[i], exp_w2b[i], t=t_moe, fc=t["fc_moe"])
            x2 = _combine(pos.reshape(n // t["tc"], 1, 2 * t["tc"]), xn, route, y, tc=t["tc"])
        x = x2.reshape(b, s, d)
    return x


def kernel(x, g_mix, w_in, b_gate, b_forget, pool_w, pool_scale, conv_w, conv_b, conv_ln_g, conv_ln_b,
           q_norm_g, k_norm_g, w_br_pool, w_br_conv, w_br_attn, w_out, g_ffn, ffn_w1, ffn_w3, ffn_w2,
           router_w, exp_w1, exp_w3, exp_w2):
    return _forward(x, g_mix, w_in, b_gate, b_forget, pool_w, pool_scale, conv_w, conv_b, conv_ln_g, conv_ln_b,
                    q_norm_g, k_norm_g, w_br_pool, w_br_conv, w_br_attn, w_out, g_ffn, ffn_w1, ffn_w3, ffn_w2,
                    router_w, exp_w1, exp_w3, exp_w2, _tiles(x.shape[0], x.shape[1]))
```

```python
import functools

import jax
import jax.numpy as jnp
import numpy as np
from jax import lax
from jax.experimental import pallas as pl
from jax.experimental.pallas import tpu as pltpu

F32 = jnp.float32
BF16 = jnp.bfloat16

EPS = 1e-6
LOG2E = 1.4426950408889634
POOL_WINDOWS = (2, 4, 8, 16)
POOL_WIDTH = 256
POOL_GROUP_DIM = 64
CONV_WIDTH = 256
CONV_KERNEL = 31
ATTN_HEADS = 8
HEAD_DIM = 64
ATTN_WIDTH = ATTN_HEADS * HEAD_DIM
N_BRANCH = 3
N_EXPERTS = 8
LANES = 128
SUBLANES = 8
HALO = 32
VMEM_LIMIT = 56 * 1024 * 1024


def _params(*sem):
    return pltpu.CompilerParams(dimension_semantics=sem, vmem_limit_bytes=VMEM_LIMIT)


def _const_spec(shape):
    nd = len(shape)
    return pl.BlockSpec(shape, lambda *_: (0,) * nd)


def _inproj_kernel(x_ref, g_ref, wloc_ref, wqk_ref, wv_ref, wf_ref, wg_ref, bg_ref, bf_ref, qg_ref, kg_ref, bd_ref,
                   pq_ref, pk_ref, oq_ref, ok_ref, ov_ref,
                   pool_ref, z_ref, q_ref, k_ref, v_ref, gates_ref, carry_ref):
    tm = x_ref.shape[1]

    @pl.when(pl.program_id(1) == 0)
    def _():
        carry_ref[...] = jnp.zeros_like(carry_ref)

    x = x_ref[0]
    ms = jnp.mean(x * x, axis=-1, keepdims=True)
    h = (x * lax.rsqrt(ms + EPS) * g_ref[...]).astype(BF16)

    loc = jnp.dot(h, wloc_ref[...], preferred_element_type=F32)
    pool_ref[0] = loc[:, :POOL_WIDTH]
    z_ref[0] = loc[:, POOL_WIDTH:POOL_WIDTH + CONV_WIDTH] * jax.nn.sigmoid(loc[:, POOL_WIDTH + CONV_WIDTH:])

    fl = jnp.dot(h, wf_ref[...], preferred_element_type=F32) + bf_ref[...]
    cs = jnp.minimum(fl, 0.0) - jnp.log1p(jnp.exp(-jnp.abs(fl)))
    row = lax.broadcasted_iota(jnp.int32, cs.shape, 0)
    shift = 1
    while shift < tm:
        cs = cs + jnp.where(row >= shift, pltpu.roll(cs, shift, axis=0), 0.0)
        shift *= 2
    cs = cs + carry_ref[0:1, :]
    carry_ref[0:1, :] = cs[tm - 1:tm, :]
    c2 = cs * LOG2E
    hi = c2.astype(BF16)
    rem = c2 - hi.astype(F32)
    mid = rem.astype(BF16)
    lo = (rem - mid.astype(F32)).astype(BF16)
    lane = lax.broadcasted_iota(jnp.int32, cs.shape, 1)
    cpack = jnp.where(lane < ATTN_HEADS, hi, jnp.where(lane < 2 * ATTN_HEADS, mid, lo))

    qk = jnp.dot(h, wqk_ref[...], preferred_element_type=F32)
    q = qk[:, :ATTN_WIDTH]
    k = qk[:, ATTN_WIDTH:]
    qss = jnp.dot((q * q).astype(BF16), bd_ref[...], preferred_element_type=F32)
    kss = jnp.dot((k * k).astype(BF16), bd_ref[...], preferred_element_type=F32)
    qn = (q * lax.rsqrt(qss + EPS) * qg_ref[...]).astype(BF16)
    kn = (k * lax.rsqrt(kss + EPS) * kg_ref[...]).astype(BF16)
    q_ref[0] = (jnp.dot(jnp.concatenate([qn, cpack], axis=1), pq_ref[...], preferred_element_type=F32)
                + oq_ref[...]).astype(BF16)
    k_ref[0] = (jnp.dot(jnp.concatenate([kn, cpack], axis=1), pk_ref[...], preferred_element_type=F32)
                + ok_ref[...]).astype(BF16)
    v_ref[0] = (jnp.dot(h, wv_ref[...], preferred_element_type=F32) + ov_ref[...]).astype(BF16)

    gl = jnp.dot(h, wg_ref[...], preferred_element_type=F32) + bg_ref[...]
    gates_ref[0] = jax.nn.sigmoid(gl).astype(BF16)


def _inproj(x, *consts, tm):
    b, s, d = x.shape
    grid = (b, s // tm)
    row = lambda w: pl.BlockSpec((1, tm, w), lambda bi, si: (bi, si, 0))
    head_w = ATTN_HEADS * LANES
    out_shape = (
        jax.ShapeDtypeStruct((b, s, POOL_WIDTH), F32),
        jax.ShapeDtypeStruct((b, s, CONV_WIDTH), F32),
        jax.ShapeDtypeStruct((b, s, head_w), BF16),
        jax.ShapeDtypeStruct((b, s, head_w), BF16),
        jax.ShapeDtypeStruct((b, s, head_w), BF16),
        jax.ShapeDtypeStruct((b, s, N_BRANCH * d), BF16),
    )
    return pl.pallas_call(
        _inproj_kernel,
        grid=grid,
        in_specs=[row(d)] + [_const_spec(a.shape) for a in consts],
        out_specs=(row(POOL_WIDTH), row(CONV_WIDTH), row(head_w), row(head_w), row(head_w), row(N_BRANCH * d)),
        out_shape=out_shape,
        scratch_shapes=[pltpu.VMEM((SUBLANES, LANES), F32)],
        compiler_params=_params("arbitrary", "arbitrary"),
        name="inproj",
    )(x, *consts)


def _local_kernel(p_ref, z_ref, wpd_ref, pscale_ref, wlane_ref, cw_ref, cb_ref, lng_ref, lnb_ref,
                  yp_ref, yc_ref, pbuf, zbuf):
    tm = p_ref.shape[1]
    si = pl.program_id(1)

    @pl.when(si == 0)
    def _():
        pbuf[0:HALO, :] = jnp.zeros((HALO, POOL_WIDTH), F32)
        zbuf[0:HALO, :] = jnp.zeros((HALO, CONV_WIDTH), F32)

    p = p_ref[0]
    pbuf[HALO:HALO + tm, :] = p
    zbuf[HALO:HALO + tm, :] = z_ref[0]

    wlane = wlane_ref[...]
    acc = p
    pooled_sum = jnp.zeros_like(p)
    for j in range(1, max(POOL_WINDOWS)):
        acc = acc + pbuf[HALO - j:HALO - j + tm, :]
        if (j + 1) in POOL_WINDOWS:
            pooled_sum = jnp.where(wlane == float(j + 1), acc, pooled_sum)
    t1 = (lax.broadcasted_iota(jnp.int32, (tm, 1), 0) + si * tm + 1).astype(F32)
    count = jnp.minimum(t1, wlane)
    pooled = pooled_sum / count - p
    yp = jnp.dot(pooled.astype(BF16), wpd_ref[...], preferred_element_type=F32) * pscale_ref[...]
    yp_ref[0] = yp.astype(BF16)

    conv = jnp.zeros((tm, CONV_WIDTH), F32) + cb_ref[...]
    for kk in range(CONV_KERNEL):
        off = HALO - (CONV_KERNEL - 1) + kk
        conv = conv + zbuf[off:off + tm, :] * cw_ref[kk:kk + 1, :]
    mu = jnp.mean(conv, axis=-1, keepdims=True)
    cen = conv - mu
    var = jnp.mean(cen * cen, axis=-1, keepdims=True)
    y = cen * lax.rsqrt(var + EPS) * lng_ref[...] + lnb_ref[...]
    yc_ref[0] = (y * jax.nn.sigmoid(y)).astype(BF16)

    pbuf[0:HALO, :] = pbuf[tm:tm + HALO, :]
    zbuf[0:HALO, :] = zbuf[tm:tm + HALO, :]


def _local(p, z, wpd, pscale, wlane, cw, cb, lng, lnb, *, tm):
    b, s, _ = p.shape
    row = lambda w: pl.BlockSpec((1, tm, w), lambda bi, si: (bi, si, 0))
    return pl.pallas_call(
        _local_kernel,
        grid=(b, s // tm),
        in_specs=[row(POOL_WIDTH), row(CONV_WIDTH)]
        + [_const_spec(a.shape) for a in (wpd, pscale, wlane, cw, cb, lng, lnb)],
        out_specs=(row(POOL_WIDTH), row(CONV_WIDTH)),
        out_shape=(jax.ShapeDtypeStruct((b, s, POOL_WIDTH), BF16), jax.ShapeDtypeStruct((b, s, CONV_WIDTH), BF16)),
        scratch_shapes=[pltpu.VMEM((HALO + tm, POOL_WIDTH), F32), pltpu.VMEM((HALO + tm, CONV_WIDTH), F32)],
        compiler_params=_params("arbitrary", "arbitrary"),
        name="local_mixers",
    )(p, z, wpd, pscale, wlane, cw, cb, lng, lnb)


def _attn_kernel(q_ref, k_ref, v_ref, o_ref, m_ref, acc_ref, s_ref):
    tq = q_ref.shape[1]
    qi = pl.program_id(2)
    q = q_ref[0]
    m_ref[...] = jnp.full(m_ref.shape, -jnp.inf, F32)
    acc_ref[...] = jnp.zeros(acc_ref.shape, F32)

    def scores(j):
        koff = pl.multiple_of(j * tq, tq)
        return lax.dot_general(q, k_ref[0, pl.ds(koff, tq), :], (((1,), (1,)), ((), ())),
                               preferred_element_type=F32)

    def consume(j, slot, diagonal=False):
        koff = pl.multiple_of(j * tq, tq)
        s = s_ref[slot]
        if diagonal:
            causal = lax.broadcasted_iota(jnp.int32, (tq, tq), 1) <= lax.broadcasted_iota(jnp.int32, (tq, tq), 0)
            s = jnp.where(causal, s, -jnp.inf)
        m_old = m_ref[...]
        m_new = jnp.maximum(m_old, jnp.max(s, axis=-1, keepdims=True))
        p = jnp.exp2(s - jnp.concatenate([m_new] * (tq // LANES), axis=1))
        alpha = jnp.exp2(m_old - m_new)
        acc_ref[...] = alpha * acc_ref[...] + jnp.dot(p.astype(BF16), v_ref[0, pl.ds(koff, tq), :],
                                                      preferred_element_type=F32)
        m_ref[...] = m_new

    s_ref[0] = scores(0)

    def pair(i, carry):
        s_ref[1] = scores(2 * i + 1)
        consume(2 * i, 0)
        s_ref[0] = scores(2 * i + 2)
        consume(2 * i + 1, 1)
        return carry

    lax.fori_loop(0, qi // 2, pair, 0)

    @pl.when(qi % 2 == 1)
    def _():
        s_ref[1] = scores(qi)
        consume(qi - 1, 0)
        consume(qi, 1, diagonal=True)

    @pl.when(qi % 2 == 0)
    def _():
        consume(qi, 0, diagonal=True)

    acc = acc_ref[...]
    denom = pltpu.roll(acc, HEAD_DIM, axis=1)
    lane = lax.broadcasted_iota(jnp.int32, acc.shape, 1)
    o_ref[0] = jnp.where(lane < HEAD_DIM, acc / denom, 0.0).astype(BF16)


def _attention(q, k, v, *, tq):
    b, s, _ = q.shape
    return pl.pallas_call(
        _attn_kernel,
        grid=(b, ATTN_HEADS, s // tq),
        in_specs=[
            pl.BlockSpec((1, tq, LANES), lambda bi, hi, qi: (bi, qi, hi)),
            pl.BlockSpec((1, s, LANES), lambda bi, hi, qi: (bi, 0, hi)),
            pl.BlockSpec((1, s, LANES), lambda bi, hi, qi: (bi, 0, hi)),
        ],
        out_specs=pl.BlockSpec((1, tq, LANES), lambda bi, hi, qi: (bi, qi, hi)),
        out_shape=jax.ShapeDtypeStruct((b, s, ATTN_HEADS * LANES), BF16),
        scratch_shapes=[pltpu.VMEM((tq, LANES), F32), pltpu.VMEM((tq, LANES), F32), pltpu.VMEM((2, tq, tq), F32)],
        compiler_params=_params("arbitrary", "arbitrary", "arbitrary"),
        name="fox_attention",
    )(q, k, v)


def _merge_core(x_ref, yp_ref, yc_ref, ya_ref, gates_ref, wp_ref, wc_ref, wa_ref, wo_ref, gf_ref):
    d = x_ref.shape[1]
    gates = gates_ref[...]
    merged = (gates[:, :d].astype(F32) * jnp.dot(yp_ref[...], wp_ref[...], preferred_element_type=F32)
              + gates[:, d:2 * d].astype(F32) * jnp.dot(yc_ref[...], wc_ref[...], preferred_element_type=F32)
              + gates[:, 2 * d:].astype(F32) * jnp.dot(ya_ref[...], wa_ref[...], preferred_element_type=F32))
    xn = x_ref[...] + jnp.dot(merged.astype(BF16), wo_ref[...], preferred_element_type=F32)
    ms = jnp.mean(xn * xn, axis=-1, keepdims=True)
    h2 = xn * lax.rsqrt(ms + EPS) * gf_ref[...]
    return xn, h2


def _merge_dense_kernel(x_ref, yp_ref, yc_ref, ya_ref, gates_ref, wp_ref, wc_ref, wa_ref, wo_ref, gf_ref,
                        xn_ref, h2_ref):
    xn, h2 = _merge_core(x_ref, yp_ref, yc_ref, ya_ref, gates_ref, wp_ref, wc_ref, wa_ref, wo_ref, gf_ref)
    xn_ref[...] = xn
    h2_ref[...] = h2.astype(BF16)


def _merge_moe_kernel(x_ref, yp_ref, yc_ref, ya_ref, gates_ref, wp_ref, wc_ref, wa_ref, wo_ref, gf_ref,
                      wr_ref, tri_ref, xn_ref, h2t_ref, route_ref, counts_ref, carry_ref):
    tm, d = x_ref.shape

    @pl.when(pl.program_id(0) == 0)
    def _():
        carry_ref[...] = jnp.zeros_like(carry_ref)

    xn, h2 = _merge_core(x_ref, yp_ref, yc_ref, ya_ref, gates_ref, wp_ref, wc_ref, wa_ref, wo_ref, gf_ref)
    xn_ref[...] = xn
    for c in range(d // LANES):
        h2t_ref[pl.ds(c, tm, stride=SUBLANES), :] = h2[:, c * LANES:(c + 1) * LANES]

    logits = jnp.dot(h2, wr_ref[...], preferred_element_type=F32, precision=lax.Precision.HIGHEST)
    lane = lax.broadcasted_iota(jnp.int32, (tm, LANES), 1)
    lanef = lane.astype(F32)
    lg = jnp.where(lane < N_EXPERTS, logits, -jnp.inf)
    m1 = jnp.max(lg, axis=-1, keepdims=True)
    i1 = jnp.min(jnp.where(lg == m1, lanef, float(LANES)), axis=-1, keepdims=True)
    lg2 = jnp.where(lanef == i1, -jnp.inf, lg)
    m2 = jnp.max(lg2, axis=-1, keepdims=True)
    i2 = jnp.min(jnp.where(lg2 == m2, lanef, float(LANES)), axis=-1, keepdims=True)
    e2 = jnp.exp(m2 - m1)
    w1 = 1.0 / (1.0 + e2)
    w2 = e2 / (1.0 + e2)

    is1 = lanef == i1
    is2 = lanef == i2
    sel = jnp.where(is1 | is2, 1.0, 0.0)
    rank = jnp.dot(tri_ref[...], sel.astype(BF16), preferred_element_type=F32) + carry_ref[0:1, :]
    r1 = jnp.sum(jnp.where(is1, rank, 0.0), axis=-1, keepdims=True)
    r2 = jnp.sum(jnp.where(is2, rank, 0.0), axis=-1, keepdims=True)
    total = rank[tm - 1:tm, :] + sel[tm - 1:tm, :]
    carry_ref[0:1, :] = total
    counts_ref[...] = jnp.broadcast_to(total, counts_ref.shape)

    route = jnp.zeros((tm, LANES), F32)
    for li, val in enumerate((i1, i2, w1, w2, r1, r2)):
        route = jnp.where(lane == li, val, route)
    route_ref[...] = route


def _merge(x2, yp, yc, ya, gates, wp, wc, wa, wo, gf, wr=None, tri=None, *, tm):
    n, d = x2.shape
    row = lambda w: pl.BlockSpec((tm, w), lambda i: (i, 0))
    ins = [x2, yp, yc, ya, gates, wp, wc, wa, wo, gf]
    in_specs = [row(d), row(POOL_WIDTH), row(CONV_WIDTH), row(ya.shape[1]), row(N_BRANCH * d)] + [
        _const_spec(a.shape) for a in (wp, wc, wa, wo, gf)]
    if wr is None:
        return pl.pallas_call(
            _merge_dense_kernel,
            grid=(n // tm,),
            in_specs=in_specs,
            out_specs=(row(d), row(d)),
            out_shape=(jax.ShapeDtypeStruct((n, d), F32), jax.ShapeDtypeStruct((n, d), BF16)),
            compiler_params=_params("arbitrary"),
            name="merge_dense",
        )(*ins)
    return pl.pallas_call(
        _merge_moe_kernel,
        grid=(n // tm,),
        in_specs=in_specs + [_const_spec(wr.shape), _const_spec(tri.shape)],
        out_specs=(row(d), pl.BlockSpec((tm * SUBLANES, LANES), lambda i: (i, 0)), row(LANES),
                   _const_spec((SUBLANES, LANES))),
        out_shape=(jax.ShapeDtypeStruct((n, d), F32), jax.ShapeDtypeStruct((n * SUBLANES, LANES), F32),
                   jax.ShapeDtypeStruct((n, LANES), F32), jax.ShapeDtypeStruct((SUBLANES, LANES), F32)),
        scratch_shapes=[pltpu.VMEM((SUBLANES, LANES), F32)],
        compiler_params=_params("arbitrary"),
        name="merge_moe",
    )(*ins, wr, tri)


def _swiglu_chunk(xb, w1_ref, w3_ref, w2_ref):
    a = jnp.dot(xb, w1_ref[0], preferred_element_type=F32)
    b = jnp.dot(xb, w3_ref[0], preferred_element_type=F32)
    hmid = (a * jax.nn.sigmoid(a) * b).astype(BF16)
    return jnp.dot(hmid, w2_ref[0], preferred_element_type=F32)


def _ffn_dense_kernel(x_ref, h_ref, w1_ref, w3_ref, w2_ref, o_ref, acc_ref):
    f = pl.program_id(1)

    @pl.when(f == 0)
    def _():
        acc_ref[...] = x_ref[...]

    acc_ref[...] += _swiglu_chunk(h_ref[...], w1_ref, w3_ref, w2_ref)

    @pl.when(f == pl.num_programs(1) - 1)
    def _():
        o_ref[...] = acc_ref[...]


def _ffn_dense(x2, h2, w1, w3, w2, *, tm, fc):
    n, d = x2.shape
    ff = w1.shape[2]
    row = lambda: pl.BlockSpec((tm, d), lambda i, f: (i, 0))
    return pl.pallas_call(
        _ffn_dense_kernel,
        grid=(n // tm, ff // fc),
        in_specs=[row(), row(),
                  pl.BlockSpec((1, d, fc), lambda i, f: (0, 0, f)),
                  pl.BlockSpec((1, d, fc), lambda i, f: (0, 0, f)),
                  pl.BlockSpec((1, fc, d), lambda i, f: (0, f, 0))],
        out_specs=row(),
        out_shape=jax.ShapeDtypeStruct((n, d), F32),
        scratch_shapes=[pltpu.VMEM((tm, d), F32)],
        compiler_params=_params("arbitrary", "arbitrary"),
        name="ffn_dense",
    )(x2, h2, w1, w3, w2)


def _ffn_grouped_kernel(te_ref, nused_ref, xs_ref, w1_ref, w3_ref, w2_ref, y_ref, xb_ref, acc_ref):
    t, d = xb_ref.shape
    i = pl.program_id(0)
    f = pl.program_id(1)
    live = i < nused_ref[0]

    @pl.when(f == 0)
    def _():
        for c in range(d // LANES):
            xb_ref[:, c * LANES:(c + 1) * LANES] = xs_ref[pl.ds(c, t, stride=SUBLANES), :].astype(BF16)
        acc_ref[...] = jnp.zeros_like(acc_ref)

    @pl.when(live)
    def _():
        acc_ref[...] += _swiglu_chunk(xb_ref[...], w1_ref, w3_ref, w2_ref)

    @pl.when(f == pl.num_programs(1) - 1)
    def _():
        for c in range(d // LANES):
            y_ref[pl.ds(c, t, stride=SUBLANES), :] = acc_ref[:, c * LANES:(c + 1) * LANES]


def _ffn_grouped(tile_expert, n_used, xs, w1, w3, w2, *, t, fc):
    rows, _ = xs.shape
    _, d, ff = w1.shape
    n_tiles = rows // (t * SUBLANES)
    blk = pl.BlockSpec((t * SUBLANES, LANES), lambda i, f, te, nu: (i, 0))
    grid_spec = pltpu.PrefetchScalarGridSpec(
        num_scalar_prefetch=2,
        grid=(n_tiles, ff // fc),
        in_specs=[blk,
                  pl.BlockSpec((1, d, fc), lambda i, f, te, nu: (te[i], 0, f)),
                  pl.BlockSpec((1, d, fc), lambda i, f, te, nu: (te[i], 0, f)),
                  pl.BlockSpec((1, fc, d), lambda i, f, te, nu: (te[i], f, 0))],
        out_specs=blk,
        scratch_shapes=[pltpu.VMEM((t, d), BF16), pltpu.VMEM((t, d), F32)],
    )
    return pl.pallas_call(
        _ffn_grouped_kernel,
        grid_spec=grid_spec,
        out_shape=jax.ShapeDtypeStruct(xs.shape, F32),
        compiler_params=_params("arbitrary", "arbitrary"),
        name="ffn_grouped",
    )(tile_expert, n_used, xs, w1, w3, w2)


def _dispatch_kernel(pos_ref, h_ref, xs_in, xs_hbm, sem):
    del xs_in
    td = pos_ref.shape[2] // 2

    def row_copy(tt, kk):
        slot = pos_ref[0, 0, 2 * tt + kk]
        return pltpu.make_async_copy(h_ref.at[pl.ds(tt * SUBLANES, SUBLANES)],
                                     xs_hbm.at[pl.ds(slot * SUBLANES, SUBLANES)], sem)

    def issue(tt, carry):
        for kk in range(2):
            row_copy(tt, kk).start()
        return carry

    lax.fori_loop(0, td, issue, 0, unroll=4)

    def drain(tt, carry):
        for kk in range(2):
            row_copy(tt, kk).wait()
        return carry

    lax.fori_loop(0, td, drain, 0, unroll=4)


def _dispatch(pos2, h2t, n_slots, *, td):
    n = h2t.shape[0] // SUBLANES
    xs0 = jnp.zeros((n_slots * SUBLANES, LANES), F32)
    return pl.pallas_call(
        _dispatch_kernel,
        grid=(n // td,),
        in_specs=[pl.BlockSpec((1, 1, 2 * td), lambda i: (i, 0, 0), memory_space=pltpu.SMEM),
                  pl.BlockSpec((td * SUBLANES, LANES), lambda i: (i, 0)),
                  pl.BlockSpec(memory_space=pl.ANY)],
        out_specs=pl.BlockSpec(memory_space=pl.ANY),
        out_shape=jax.ShapeDtypeStruct(xs0.shape, F32),
        scratch_shapes=[pltpu.SemaphoreType.DMA(())],
        input_output_aliases={2: 0},
        compiler_params=_params("arbitrary"),
        name="moe_dispatch",
    )(pos2, h2t, xs0)


def _combine_kernel(pos_ref, x_ref, route_ref, y_hbm, o_ref, buf, sem):
    tc, d = x_ref.shape

    def row_copy(tt, kk):
        slot = pos_ref[0, 0, 2 * tt + kk]
        return pltpu.make_async_copy(y_hbm.at[pl.ds(slot * SUBLANES, SUBLANES)],
                                     buf.at[kk, pl.ds(tt * SUBLANES, SUBLANES)], sem)

    def issue(tt, carry):
        for kk in range(2):
            row_copy(tt, kk).start()
        return carry

    lax.fori_loop(0, tc, issue, 0)

    def drain(tt, carry):
        for kk in range(2):
            row_copy(tt, kk).wait()
        return carry

    lax.fori_loop(0, tc, drain, 0)

    route = route_ref[...]
    w1 = route[:, 2:3]
    w2 = route[:, 3:4]
    for c in range(d // LANES):
        y1 = buf[0, pl.ds(c, tc, stride=SUBLANES), :]
        y2 = buf[1, pl.ds(c, tc, stride=SUBLANES), :]
        o_ref[:, c * LANES:(c + 1) * LANES] = x_ref[:, c * LANES:(c + 1) * LANES] + (w1 * y1 + w2 * y2)


def _combine(pos2, x2, route, y, *, tc):
    n, d = x2.shape
    return pl.pallas_call(
        _combine_kernel,
        grid=(n // tc,),
        in_specs=[pl.BlockSpec((1, 1, 2 * tc), lambda i: (i, 0, 0), memory_space=pltpu.SMEM),
                  pl.BlockSpec((tc, d), lambda i: (i, 0)),
                  pl.BlockSpec((tc, LANES), lambda i: (i, 0)),
                  pl.BlockSpec(memory_space=pl.ANY)],
        out_specs=pl.BlockSpec((tc, d), lambda i: (i, 0)),
        out_shape=jax.ShapeDtypeStruct((n, d), F32),
        scratch_shapes=[pltpu.VMEM((2, tc * SUBLANES, LANES), F32), pltpu.SemaphoreType.DMA(())],
        compiler_params=_params("arbitrary"),
        name="moe_combine",
    )(pos2, x2, route, y)


def _tiles(b, s):
    n = b * s
    pick = lambda full, unit: full if unit % full == 0 else unit
    return dict(
        tm_in=pick(512, s), tm_loc=pick(512, s), tq=pick(512, s), tm_merge=pick(512, n),
        tm_ffn=pick(1024, n), fc_dense=1408, t_moe=pick(1024, n) if n >= 8192 else 256, fc_moe=512,
        td=pick(512, n), tc=pick(256, n),
    )


def _head_layout_constants():
    pq = np.zeros((ATTN_WIDTH + LANES, ATTN_HEADS * LANES), np.float32)
    pk = np.zeros_like(pq)
    oq = np.zeros((1, ATTN_HEADS * LANES), np.float32)
    ok = np.zeros_like(oq)
    ov = np.zeros_like(oq)
    for h in range(ATTN_HEADS):
        base = h * LANES
        for j in range(HEAD_DIM):
            pq[h * HEAD_DIM + j, base + j] = 1.0
            pk[h * HEAD_DIM + j, base + j] = 1.0
        for part in range(3):
            pq[ATTN_WIDTH + part * ATTN_HEADS + h, base + HEAD_DIM + part] = 1.0
            pk[ATTN_WIDTH + part * ATTN_HEADS + h, base + HEAD_DIM + 3 + part] = -1.0
            oq[0, base + HEAD_DIM + 3 + part] = 1.0
            ok[0, base + HEAD_DIM + part] = 1.0
        ov[0, base + HEAD_DIM:base + LANES] = 1.0
    return (jnp.asarray(pq, BF16), jnp.asarray(pk, BF16), jnp.asarray(oq), jnp.asarray(ok), jnp.asarray(ov))


def _block_diag(blocks):
    g, r, c = blocks.shape
    eye = jnp.eye(g, dtype=blocks.dtype)
    return (eye[:, None, :, None] * blocks[:, :, None, :]).reshape(g * r, g * c)


def _forward(x, g_mix, w_in, b_gate, b_forget, pool_w, pool_scale, conv_w, conv_b, conv_ln_g, conv_ln_b,
             q_norm_g, k_norm_g, w_br_pool, w_br_conv, w_br_attn, w_out, g_ffn, ffn_w1, ffn_w3, ffn_w2,
             router_w, exp_w1, exp_w3, exp_w2, tiles):
    b, s, d = x.shape
    n = b * s
    depth = w_in.shape[0]
    t = tiles

    off_conv = POOL_WIDTH
    off_q = off_conv + 2 * CONV_WIDTH
    off_f = off_q + 3 * ATTN_WIDTH
    off_g = off_f + ATTN_HEADS

    wlane = jnp.repeat(jnp.asarray(POOL_WINDOWS, F32), POOL_GROUP_DIM)[None, :]
    bd = _block_diag(jnp.full((ATTN_HEADS, HEAD_DIM, HEAD_DIM), 1.0 / HEAD_DIM, F32)).astype(BF16)
    tri = (lax.broadcasted_iota(jnp.int32, (t["tm_merge"], t["tm_merge"]), 1)
           < lax.broadcasted_iota(jnp.int32, (t["tm_merge"], t["tm_merge"]), 0)).astype(BF16)
    ffn_w1b, ffn_w3b, ffn_w2b = ffn_w1.astype(BF16), ffn_w3.astype(BF16), ffn_w2.astype(BF16)
    exp_w1b, exp_w3b, exp_w2b = exp_w1.astype(BF16), exp_w3.astype(BF16), exp_w2.astype(BF16)

    t_moe = t["t_moe"]
    n_tiles = (2 * n) // t_moe + N_EXPERTS
    n_slots = n_tiles * t_moe

    pq, pk, oq, ok, ov = _head_layout_constants()

    def head_pad(w, axis):
        shp = w.shape
        w = w.reshape(shp[:axis] + (ATTN_HEADS, HEAD_DIM) + shp[axis + 1:])
        pad = [(0, 0)] * w.ndim
        pad[axis + 1] = (0, LANES - HEAD_DIM)
        return jnp.pad(w, pad).reshape(shp[:axis] + (ATTN_HEADS * LANES,) + shp[axis + 1:])

    for l in range(depth):
        wl = w_in[l]
        wloc = wl[:, :off_q].astype(BF16)
        wqk = wl[:, off_q:off_q + 2 * ATTN_WIDTH].astype(BF16)
        wv = head_pad(wl[:, off_q + 2 * ATTN_WIDTH:off_f], 1).astype(BF16)
        wf3 = jnp.pad(jnp.tile(wl[:, off_f:off_g], (1, 3)), ((0, 0), (0, LANES - 3 * ATTN_HEADS))).astype(BF16)
        bf3 = jnp.pad(jnp.tile(b_forget[l], 3), (0, LANES - 3 * ATTN_HEADS))[None, :]
        wg = wl[:, off_g:].astype(BF16)
        qg = jnp.tile(q_norm_g[l], ATTN_HEADS)[None, :] * (HEAD_DIM ** -0.5 * LOG2E)
        kg = jnp.tile(k_norm_g[l], ATTN_HEADS)[None, :]
        pool_in, z, q, k, v, gates = _inproj(
            x, g_mix[l][None, :], wloc, wqk, wv, wf3, wg, b_gate[l][None, :], bf3, qg, kg, bd, pq, pk, oq, ok, ov,
            tm=t["tm_in"])

        yp, yc = _local(pool_in, z, _block_diag(pool_w[l]).astype(BF16), pool_scale[l][None, :], wlane,
                        conv_w[l], conv_b[l][None, :], conv_ln_g[l][None, :], conv_ln_b[l][None, :],
                        tm=t["tm_loc"])

        ya = _attention(q, k, v, tq=t["tq"])

        merge_args = (x.reshape(n, d), yp.reshape(n, -1), yc.reshape(n, -1), ya.reshape(n, -1),
                      gates.reshape(n, -1), w_br_pool[l].astype(BF16), w_br_conv[l].astype(BF16),
                      head_pad(w_br_attn[l], 0).astype(BF16), w_out[l].astype(BF16), g_ffn[l][None, :])
        i = l // 2
        if l % 2 == 0:
            xn, h2 = _merge(*merge_args, tm=t["tm_merge"])
            x2 = _ffn_dense(xn, h2, ffn_w1b[i:i + 1], ffn_w3b[i:i + 1], ffn_w2b[i:i + 1],
                            tm=t["tm_ffn"], fc=t["fc_dense"])
        else:
            wr = jnp.pad(router_w[i], ((0, 0), (0, LANES - N_EXPERTS)))
            xn, h2t, route, counts = _merge(*merge_args, wr, tri, tm=t["tm_merge"])
            cnt = counts[0, :N_EXPERTS].astype(jnp.int32)
            ntile_e = (cnt + t_moe - 1) // t_moe
            ends = jnp.cumsum(ntile_e) * t_moe
            starts = ends - ntile_e * t_moe
            idx = route[:, 0:2].astype(jnp.int32)
            rank = route[:, 4:6].astype(jnp.int32)
            pos = starts[idx] + rank
            tile_first = jnp.arange(n_tiles, dtype=jnp.int32) * t_moe
            tile_expert = jnp.minimum(jnp.sum(tile_first[:, None] >= ends[None, :], axis=1),
                                      N_EXPERTS - 1).astype(jnp.int32)
            n_used = (ends[-1:] // t_moe).astype(jnp.int32)
            xs = _dispatch(pos.reshape(n // t["td"], 1, 2 * t["td"]), h2t, n_slots, td=t["td"])
            y = _ffn_grouped(tile_expert, n_used, xs, exp_w1b[i], exp_w3b[i], exp_w2b[i], t=t_moe, fc=t["fc_moe"])
            x2 = _combine(pos.reshape(n // t["tc"], 1, 2 * t["tc"]), xn, route, y, tc=t["tc"])
        x = x2.reshape(b, s, d)
    return x


def kernel(x, g_mix, w_in, b_gate, b_forget, pool_w, pool_scale, conv_w, conv_b, conv_ln_g, conv_ln_b,
           q_norm_g, k_norm_g, w_br_pool, w_br_conv, w_br_attn, w_out, g_ffn, ffn_w1, ffn_w3, ffn_w2,
           router_w, exp_w1, exp_w3, exp_w2):
    return _forward(x, g_mix, w_in, b_gate, b_forget, pool_w, pool_scale, conv_w, conv_b, conv_ln_g, conv_ln_b,
                    q_norm_g, k_norm_g, w_br_pool, w_br_conv, w_br_attn, w_out, g_ffn, ffn_w1, ffn_w3, ffn_w2,
                    router_w, exp_w1, exp_w3, exp_w2, _tiles(x.shape[0], x.shape[1]))
```

```python
import functools

import jax
import jax.numpy as jnp
import numpy as np
from jax import lax
from jax.experimental import pallas as pl
from jax.experimental.pallas import tpu as pltpu

F32 = jnp.float32
BF16 = jnp.bfloat16

EPS = 1e-6
LOG2E = 1.4426950408889634
POOL_WINDOWS = (2, 4, 8, 16)
POOL_WIDTH = 256
POOL_GROUP_DIM = 64
CONV_WIDTH = 256
CONV_KERNEL = 31
ATTN_HEADS = 8
HEAD_DIM = 64
ATTN_WIDTH = ATTN_HEADS * HEAD_DIM
N_BRANCH = 3
N_EXPERTS = 8
LANES = 128
SUBLANES = 8
HALO = 32
VMEM_LIMIT = 56 * 1024 * 1024


def _params(*sem):
    return pltpu.CompilerParams(dimension_semantics=sem, vmem_limit_bytes=VMEM_LIMIT)


def _const_spec(shape):
    nd = len(shape)
    return pl.BlockSpec(shape, lambda *_: (0,) * nd)


def _inproj_kernel(x_ref, g_ref, wloc_ref, wqk_ref, wv_ref, wf_ref, wg_ref, bg_ref, bf_ref, qg_ref, kg_ref, bd_ref,
                   pq_ref, pk_ref, oq_ref, ok_ref, ov_ref,
                   pool_ref, z_ref, q_ref, k_ref, v_ref, gates_ref, carry_ref):
    tm = x_ref.shape[1]

    @pl.when(pl.program_id(1) == 0)
    def _():
        carry_ref[...] = jnp.zeros_like(carry_ref)

    x = x_ref[0]
    ms = jnp.mean(x * x, axis=-1, keepdims=True)
    h = (x * lax.rsqrt(ms + EPS) * g_ref[...]).astype(BF16)

    loc = jnp.dot(h, wloc_ref[...], preferred_element_type=F32)
    pool_ref[0] = loc[:, :POOL_WIDTH]
    z_ref[0] = loc[:, POOL_WIDTH:POOL_WIDTH + CONV_WIDTH] * jax.nn.sigmoid(loc[:, POOL_WIDTH + CONV_WIDTH:])

    fl = jnp.dot(h, wf_ref[...], preferred_element_type=F32) + bf_ref[...]
    cs = jnp.minimum(fl, 0.0) - jnp.log1p(jnp.exp(-jnp.abs(fl)))
    row = lax.broadcasted_iota(jnp.int32, cs.shape, 0)
    shift = 1
    while shift < tm:
        cs = cs + jnp.where(row >= shift, pltpu.roll(cs, shift, axis=0), 0.0)
        shift *= 2
    cs = cs + carry_ref[0:1, :]
    carry_ref[0:1, :] = cs[tm - 1:tm, :]
    c2 = cs * LOG2E
    hi = c2.astype(BF16)
    rem = c2 - hi.astype(F32)
    mid = rem.astype(BF16)
    lo = (rem - mid.astype(F32)).astype(BF16)
    lane = lax.broadcasted_iota(jnp.int32, cs.shape, 1)
    cpack = jnp.where(lane < ATTN_HEADS, hi, jnp.where(lane < 2 * ATTN_HEADS, mid, lo))

    qk = jnp.dot(h, wqk_ref[...], preferred_element_type=F32)
    q = qk[:, :ATTN_WIDTH]
    k = qk[:, ATTN_WIDTH:]
    qss = jnp.dot((q * q).astype(BF16), bd_ref[...], preferred_element_type=F32)
    kss = jnp.dot((k * k).astype(BF16), bd_ref[...], preferred_element_type=F32)
    qn = (q * lax.rsqrt(qss + EPS) * qg_ref[...]).astype(BF16)
    kn = (k * lax.rsqrt(kss + EPS) * kg_ref[...]).astype(BF16)
    q_ref[0] = (jnp.dot(jnp.concatenate([qn, cpack], axis=1), pq_ref[...], preferred_element_type=F32)
                + oq_ref[...]).astype(BF16)
    k_ref[0] = (jnp.dot(jnp.concatenate([kn, cpack], axis=1), pk_ref[...], preferred_element_type=F32)
                + ok_ref[...]).astype(BF16)
    v_ref[0] = (jnp.dot(h, wv_ref[...], preferred_element_type=F32) + ov_ref[...]).astype(BF16)

    gl = jnp.dot(h, wg_ref[...], preferred_element_type=F32) + bg_ref[...]
    gates_ref[0] = jax.nn.sigmoid(gl).astype(BF16)


def _inproj(x, *consts, tm):
    b, s, d = x.shape
    grid = (b, s // tm)
    row = lambda w: pl.BlockSpec((1, tm, w), lambda bi, si: (bi, si, 0))
    head_w = ATTN_HEADS * LANES
    out_shape = (
        jax.ShapeDtypeStruct((b, s, POOL_WIDTH), F32),
        jax.ShapeDtypeStruct((b, s, CONV_WIDTH), F32),
        jax.ShapeDtypeStruct((b, s, head_w), BF16),
        jax.ShapeDtypeStruct((b, s, head_w), BF16),
        jax.ShapeDtypeStruct((b, s, head_w), BF16),
        jax.ShapeDtypeStruct((b, s, N_BRANCH * d), BF16),
    )
    return pl.pallas_call(
        _inproj_kernel,
        grid=grid,
        in_specs=[row(d)] + [_const_spec(a.shape) for a in consts],
        out_specs=(row(POOL_WIDTH), row(CONV_WIDTH), row(head_w), row(head_w), row(head_w), row(N_BRANCH * d)),
        out_shape=out_shape,
        scratch_shapes=[pltpu.VMEM((SUBLANES, LANES), F32)],
        compiler_params=_params("arbitrary", "arbitrary"),
        name="inproj",
    )(x, *consts)


def _local_kernel(p_ref, z_ref, wpd_ref, pscale_ref, wlane_ref, cw_ref, cb_ref, lng_ref, lnb_ref,
                  yp_ref, yc_ref, pbuf, zbuf):
    tm = p_ref.shape[1]
    si = pl.program_id(1)

    @pl.when(si == 0)
    def _():
        pbuf[0:HALO, :] = jnp.zeros((HALO, POOL_WIDTH), F32)
        zbuf[0:HALO, :] = jnp.zeros((HALO, CONV_WIDTH), F32)

    p = p_ref[0]
    pbuf[HALO:HALO + tm, :] = p
    zbuf[HALO:HALO + tm, :] = z_ref[0]

    wlane = wlane_ref[...]
    acc = p
    pooled_sum = jnp.zeros_like(p)
    for j in range(1, max(POOL_WINDOWS)):
        acc = acc + pbuf[HALO - j:HALO - j + tm, :]
        if (j + 1) in POOL_WINDOWS:
            pooled_sum = jnp.where(wlane == float(j + 1), acc, pooled_sum)
    t1 = (lax.broadcasted_iota(jnp.int32, (tm, 1), 0) + si * tm + 1).astype(F32)
    count = jnp.minimum(t1, wlane)
    pooled = pooled_sum / count - p
    yp = jnp.dot(pooled.astype(BF16), wpd_ref[...], preferred_element_type=F32) * pscale_ref[...]
    yp_ref[0] = yp.astype(BF16)

    conv = jnp.zeros((tm, CONV_WIDTH), F32) + cb_ref[...]
    for kk in range(CONV_KERNEL):
        off = HALO - (CONV_KERNEL - 1) + kk
        conv = conv + zbuf[off:off + tm, :] * cw_ref[kk:kk + 1, :]
    mu = jnp.mean(conv, axis=-1, keepdims=True)
    cen = conv - mu
    var = jnp.mean(cen * cen, axis=-1, keepdims=True)
    y = cen * lax.rsqrt(var + EPS) * lng_ref[...] + lnb_ref[...]
    yc_ref[0] = (y * jax.nn.sigmoid(y)).astype(BF16)

    pbuf[0:HALO, :] = pbuf[tm:tm + HALO, :]
    zbuf[0:HALO, :] = zbuf[tm:tm + HALO, :]


def _local(p, z, wpd, pscale, wlane, cw, cb, lng, lnb, *, tm):
    b, s, _ = p.shape
    row = lambda w: pl.BlockSpec((1, tm, w), lambda bi, si: (bi, si, 0))
    return pl.pallas_call(
        _local_kernel,
        grid=(b, s // tm),
        in_specs=[row(POOL_WIDTH), row(CONV_WIDTH)]
        + [_const_spec(a.shape) for a in (wpd, pscale, wlane, cw, cb, lng, lnb)],
        out_specs=(row(POOL_WIDTH), row(CONV_WIDTH)),
        out_shape=(jax.ShapeDtypeStruct((b, s, POOL_WIDTH), BF16), jax.ShapeDtypeStruct((b, s, CONV_WIDTH), BF16)),
        scratch_shapes=[pltpu.VMEM((HALO + tm, POOL_WIDTH), F32), pltpu.VMEM((HALO + tm, CONV_WIDTH), F32)],
        compiler_params=_params("arbitrary", "arbitrary"),
        name="local_mixers",
    )(p, z, wpd, pscale, wlane, cw, cb, lng, lnb)


def _attn_kernel(q_ref, k_ref, v_ref, o_ref, m_ref, acc_ref, s_ref, p_ref, alpha_ref):
    tq = q_ref.shape[1]
    tk = s_ref.shape[2]
    qi = pl.program_id(2)
    q = q_ref[0]
    m_ref[...] = jnp.full(m_ref.shape, -jnp.inf, F32)
    acc_ref[...] = jnp.zeros(acc_ref.shape, F32)

    def scores(j):
        koff = pl.multiple_of(j * tk, tk)
        return lax.dot_general(q, k_ref[0, pl.ds(koff, tk), :], (((1,), (1,)), ((), ())),
                               preferred_element_type=F32)

    def softmax(slot, diag_offset=None):
        s = s_ref[slot]
        if diag_offset is not None:
            kpos = lax.broadcasted_iota(jnp.int32, (tq, tk), 1) + diag_offset
            s = jnp.where(kpos <= lax.broadcasted_iota(jnp.int32, (tq, tk), 0), s, -jnp.inf)
        m_old = m_ref[...]
        m_new = jnp.maximum(m_old, jnp.max(s, axis=-1, keepdims=True))
        p_ref[slot] = jnp.exp2((s - jnp.concatenate([m_new] * (tk // LANES), axis=1)).astype(BF16))
        alpha_ref[slot] = jnp.exp2(m_old - m_new)
        m_ref[...] = m_new

    def accumulate(j, slot):
        koff = pl.multiple_of(j * tk, tk)
        acc_ref[...] = alpha_ref[slot] * acc_ref[...] + jnp.dot(p_ref[slot], v_ref[0, pl.ds(koff, tk), :],
                                                                preferred_element_type=F32)

    s_ref[0] = scores(0)
    p_ref[1] = jnp.zeros((tq, tk), BF16)
    alpha_ref[1] = jnp.ones((tq, LANES), F32)

    def pair(i, carry):
        s_ref[1] = scores(2 * i + 1)
        softmax(0)
        accumulate(jnp.maximum(2 * i - 1, 0), 1)
        s_ref[0] = scores(2 * i + 2)
        softmax(1)
        accumulate(2 * i, 0)
        return carry

    lax.fori_loop(0, qi, pair, 0)
    s_ref[1] = scores(2 * qi + 1)
    softmax(0, diag_offset=0)
    accumulate(jnp.maximum(2 * qi - 1, 0), 1)
    softmax(1, diag_offset=tk)
    accumulate(2 * qi, 0)
    accumulate(2 * qi + 1, 1)

    acc = acc_ref[...]
    denom = pltpu.roll(acc, HEAD_DIM, axis=1)
    lane = lax.broadcasted_iota(jnp.int32, acc.shape, 1)
    o_ref[0] = jnp.where(lane < HEAD_DIM, acc / denom, 0.0).astype(BF16)


def _attention(q, k, v, *, tq):
    b, s, _ = q.shape
    return pl.pallas_call(
        _attn_kernel,
        grid=(b, ATTN_HEADS, s // tq),
        in_specs=[
            pl.BlockSpec((1, tq, LANES), lambda bi, hi, qi: (bi, qi, hi)),
            pl.BlockSpec((1, s, LANES), lambda bi, hi, qi: (bi, 0, hi)),
            pl.BlockSpec((1, s, LANES), lambda bi, hi, qi: (bi, 0, hi)),
        ],
        out_specs=pl.BlockSpec((1, tq, LANES), lambda bi, hi, qi: (bi, qi, hi)),
        out_shape=jax.ShapeDtypeStruct((b, s, ATTN_HEADS * LANES), BF16),
        scratch_shapes=[pltpu.VMEM((tq, LANES), F32), pltpu.VMEM((tq, LANES), F32),
                        pltpu.VMEM((2, tq, tq // 2), F32), pltpu.VMEM((2, tq, tq // 2), BF16),
                        pltpu.VMEM((2, tq, LANES), F32)],
        compiler_params=_params("arbitrary", "arbitrary", "arbitrary"),
        name="fox_attention",
    )(q, k, v)


def _merge_core(x_ref, yp_ref, yc_ref, ya_ref, gates_ref, wp_ref, wc_ref, wa_ref, wo_ref, gf_ref):
    d = x_ref.shape[1]
    gates = gates_ref[...]
    merged = (gates[:, :d].astype(F32) * jnp.dot(yp_ref[...], wp_ref[...], preferred_element_type=F32)
              + gates[:, d:2 * d].astype(F32) * jnp.dot(yc_ref[...], wc_ref[...], preferred_element_type=F32)
              + gates[:, 2 * d:].astype(F32) * jnp.dot(ya_ref[...], wa_ref[...], preferred_element_type=F32))
    xn = x_ref[...] + jnp.dot(merged.astype(BF16), wo_ref[...], preferred_element_type=F32)
    ms = jnp.mean(xn * xn, axis=-1, keepdims=True)
    h2 = xn * lax.rsqrt(ms + EPS) * gf_ref[...]
    return xn, h2


def _merge_dense_kernel(x_ref, yp_ref, yc_ref, ya_ref, gates_ref, wp_ref, wc_ref, wa_ref, wo_ref, gf_ref,
                        xn_ref, h2_ref):
    xn, h2 = _merge_core(x_ref, yp_ref, yc_ref, ya_ref, gates_ref, wp_ref, wc_ref, wa_ref, wo_ref, gf_ref)
    xn_ref[...] = xn
    h2_ref[...] = h2.astype(BF16)


def _merge_moe_kernel(x_ref, yp_ref, yc_ref, ya_ref, gates_ref, wp_ref, wc_ref, wa_ref, wo_ref, gf_ref,
                      wr_ref, tri_ref, xn_ref, h2t_ref, route_ref, counts_ref, carry_ref):
    tm, d = x_ref.shape

    @pl.when(pl.program_id(0) == 0)
    def _():
        carry_ref[...] = jnp.zeros_like(carry_ref)

    xn, h2 = _merge_core(x_ref, yp_ref, yc_ref, ya_ref, gates_ref, wp_ref, wc_ref, wa_ref, wo_ref, gf_ref)
    xn_ref[...] = xn
    for c in range(d // LANES):
        h2t_ref[pl.ds(c, tm, stride=SUBLANES), :] = h2[:, c * LANES:(c + 1) * LANES]

    h_hi = h2.astype(BF16)
    h_lo = (h2 - h_hi.astype(F32)).astype(BF16)
    both = jnp.dot(h_hi, wr_ref[...], preferred_element_type=F32)
    logits = (both[:, :LANES] + both[:, LANES:]
              + jnp.dot(h_lo, wr_ref[:, :LANES], preferred_element_type=F32))
    lane = lax.broadcasted_iota(jnp.int32, (tm, LANES), 1)
    lanef = lane.astype(F32)
    lg = jnp.where(lane < N_EXPERTS, logits, -jnp.inf)
    m1 = jnp.max(lg, axis=-1, keepdims=True)
    i1 = jnp.min(jnp.where(lg == m1, lanef, float(LANES)), axis=-1, keepdims=True)
    lg2 = jnp.where(lanef == i1, -jnp.inf, lg)
    m2 = jnp.max(lg2, axis=-1, keepdims=True)
    i2 = jnp.min(jnp.where(lg2 == m2, lanef, float(LANES)), axis=-1, keepdims=True)
    e2 = jnp.exp(m2 - m1)
    w1 = 1.0 / (1.0 + e2)
    w2 = e2 / (1.0 + e2)

    is1 = lanef == i1
    is2 = lanef == i2
    sel = jnp.where(is1 | is2, 1.0, 0.0)
    rank = jnp.dot(tri_ref[...], sel.astype(BF16), preferred_element_type=F32) + carry_ref[0:1, :]
    r1 = jnp.sum(jnp.where(is1, rank, 0.0), axis=-1, keepdims=True)
    r2 = jnp.sum(jnp.where(is2, rank, 0.0), axis=-1, keepdims=True)
    total = rank[tm - 1:tm, :] + sel[tm - 1:tm, :]
    carry_ref[0:1, :] = total
    counts_ref[...] = jnp.broadcast_to(total, counts_ref.shape)

    route = jnp.zeros((tm, LANES), F32)
    for li, val in enumerate((i1, i2, w1, w2, r1, r2)):
        route = jnp.where(lane == li, val, route)
    route_ref[...] = route


def _merge(x2, yp, yc, ya, gates, wp, wc, wa, wo, gf, wr=None, tri=None, *, tm):
    n, d = x2.shape
    row = lambda w: pl.BlockSpec((tm, w), lambda i: (i, 0))
    ins = [x2, yp, yc, ya, gates, wp, wc, wa, wo, gf]
    in_specs = [row(d), row(POOL_WIDTH), row(CONV_WIDTH), row(ya.shape[1]), row(N_BRANCH * d)] + [
        _const_spec(a.shape) for a in (wp, wc, wa, wo, gf)]
    if wr is None:
        return pl.pallas_call(
            _merge_dense_kernel,
            grid=(n // tm,),
            in_specs=in_specs,
            out_specs=(row(d), row(d)),
            out_shape=(jax.ShapeDtypeStruct((n, d), F32), jax.ShapeDtypeStruct((n, d), BF16)),
            compiler_params=_params("arbitrary"),
            name="merge_dense",
        )(*ins)
    return pl.pallas_call(
        _merge_moe_kernel,
        grid=(n // tm,),
        in_specs=in_specs + [_const_spec(wr.shape), _const_spec(tri.shape)],
        out_specs=(row(d), pl.BlockSpec((tm * SUBLANES, LANES), lambda i: (i, 0)), row(LANES),
                   _const_spec((SUBLANES, LANES))),
        out_shape=(jax.ShapeDtypeStruct((n, d), F32), jax.ShapeDtypeStruct((n * SUBLANES, LANES), F32),
                   jax.ShapeDtypeStruct((n, LANES), F32), jax.ShapeDtypeStruct((SUBLANES, LANES), F32)),
        scratch_shapes=[pltpu.VMEM((SUBLANES, LANES), F32)],
        compiler_params=_params("arbitrary"),
        name="merge_moe",
    )(*ins, wr, tri)


FF_SUB = 256


def _swiglu_accumulate(xb, w1_ref, w3_ref, w2_ref, acc_ref):
    fc = w1_ref.shape[2]
    for c in range(fc // FF_SUB):
        cols = slice(c * FF_SUB, (c + 1) * FF_SUB)
        a = jnp.dot(xb, w1_ref[0, :, cols], preferred_element_type=F32)
        b = jnp.dot(xb, w3_ref[0, :, cols], preferred_element_type=F32)
        hmid = (a * jax.nn.sigmoid(a) * b).astype(BF16)
        acc_ref[...] += jnp.dot(hmid, w2_ref[0, cols, :], preferred_element_type=F32)


def _ffn_dense_kernel(x_ref, h_ref, w1_ref, w3_ref, w2_ref, o_ref, acc_ref):
    f = pl.program_id(1)

    @pl.when(f == 0)
    def _():
        acc_ref[...] = x_ref[...]

    _swiglu_accumulate(h_ref[...], w1_ref, w3_ref, w2_ref, acc_ref)

    @pl.when(f == pl.num_programs(1) - 1)
    def _():
        o_ref[...] = acc_ref[...]


def _ffn_dense(x2, h2, w1, w3, w2, *, tm, fc):
    n, d = x2.shape
    ff = w1.shape[2]
    row = lambda: pl.BlockSpec((tm, d), lambda i, f: (i, 0))
    once = dict(pipeline_mode=pl.Buffered(1)) if fc == ff else {}
    return pl.pallas_call(
        _ffn_dense_kernel,
        grid=(n // tm, ff // fc),
        in_specs=[row(), row(),
                  pl.BlockSpec((1, d, fc), lambda i, f: (0, 0, f), **once),
                  pl.BlockSpec((1, d, fc), lambda i, f: (0, 0, f), **once),
                  pl.BlockSpec((1, fc, d), lambda i, f: (0, f, 0), **once)],
        out_specs=row(),
        out_shape=jax.ShapeDtypeStruct((n, d), F32),
        scratch_shapes=[pltpu.VMEM((tm, d), F32)],
        compiler_params=_params("arbitrary", "arbitrary"),
        name="ffn_dense",
    )(x2, h2, w1, w3, w2)


def _ffn_grouped_kernel(te_ref, nused_ref, xs_ref, w1_ref, w3_ref, w2_ref, y_ref, xb_ref, acc_ref):
    t, d = xb_ref.shape
    i = pl.program_id(0)
    f = pl.program_id(1)
    live = i < nused_ref[0]

    @pl.when(f == 0)
    def _():
        for c in range(d // LANES):
            xb_ref[:, c * LANES:(c + 1) * LANES] = xs_ref[pl.ds(c, t, stride=SUBLANES), :].astype(BF16)
        acc_ref[...] = jnp.zeros_like(acc_ref)

    @pl.when(live)
    def _():
        _swiglu_accumulate(xb_ref[...], w1_ref, w3_ref, w2_ref, acc_ref)

    @pl.when(f == pl.num_programs(1) - 1)
    def _():
        for c in range(d // LANES):
            y_ref[pl.ds(c, t, stride=SUBLANES), :] = acc_ref[:, c * LANES:(c + 1) * LANES]


def _ffn_grouped(tile_expert, n_used, xs, w1, w3, w2, *, t, fc):
    rows, _ = xs.shape
    _, d, ff = w1.shape
    n_tiles = rows // (t * SUBLANES)
    blk = pl.BlockSpec((t * SUBLANES, LANES), lambda i, f, te, nu: (i, 0))
    grid_spec = pltpu.PrefetchScalarGridSpec(
        num_scalar_prefetch=2,
        grid=(n_tiles, ff // fc),
        in_specs=[blk,
                  pl.BlockSpec((1, d, fc), lambda i, f, te, nu: (te[i], 0, f)),
                  pl.BlockSpec((1, d, fc), lambda i, f, te, nu: (te[i], 0, f)),
                  pl.BlockSpec((1, fc, d), lambda i, f, te, nu: (te[i], f, 0))],
        out_specs=blk,
        scratch_shapes=[pltpu.VMEM((t, d), BF16), pltpu.VMEM((t, d), F32)],
    )
    return pl.pallas_call(
        _ffn_grouped_kernel,
        grid_spec=grid_spec,
        out_shape=jax.ShapeDtypeStruct(xs.shape, F32),
        compiler_params=_params("arbitrary", "arbitrary"),
        name="ffn_grouped",
    )(tile_expert, n_used, xs, w1, w3, w2)


def _dispatch_kernel(pos_ref, h_ref, xs_in, xs_hbm, sem):
    del xs_in
    td = pos_ref.shape[2] // 2

    def row_copy(tt, kk):
        slot = pos_ref[0, 0, 2 * tt + kk]
        return pltpu.make_async_copy(h_ref.at[pl.ds(tt * SUBLANES, SUBLANES)],
                                     xs_hbm.at[pl.ds(slot * SUBLANES, SUBLANES)], sem)

    def issue(tt, carry):
        for kk in range(2):
            row_copy(tt, kk).start()
        return carry

    lax.fori_loop(0, td, issue, 0, unroll=4)

    def drain(tt, carry):
        for kk in range(2):
            row_copy(tt, kk).wait()
        return carry

    lax.fori_loop(0, td, drain, 0, unroll=4)


def _dispatch(pos2, h2t, n_slots, *, td):
    n = h2t.shape[0] // SUBLANES
    xs0 = jnp.zeros((n_slots * SUBLANES, LANES), F32)
    return pl.pallas_call(
        _dispatch_kernel,
        grid=(n // td,),
        in_specs=[pl.BlockSpec((1, 1, 2 * td), lambda i: (i, 0, 0), memory_space=pltpu.SMEM),
                  pl.BlockSpec((td * SUBLANES, LANES), lambda i: (i, 0)),
                  pl.BlockSpec(memory_space=pl.ANY)],
        out_specs=pl.BlockSpec(memory_space=pl.ANY),
        out_shape=jax.ShapeDtypeStruct(xs0.shape, F32),
        scratch_shapes=[pltpu.SemaphoreType.DMA(())],
        input_output_aliases={2: 0},
        compiler_params=_params("arbitrary"),
        name="moe_dispatch",
    )(pos2, h2t, xs0)


def _combine_kernel(pos_ref, posn_ref, x_ref, route_ref, y_hbm, o_ref, buf, sem):
    tc, d = x_ref.shape
    i = pl.program_id(0)
    cur = i % 2

    def row_copy(p_ref, buf_slot, tt, kk):
        slot = p_ref[0, 0, 2 * tt + kk]
        return pltpu.make_async_copy(y_hbm.at[pl.ds(slot * SUBLANES, SUBLANES)],
                                     buf.at[buf_slot, kk, pl.ds(tt * SUBLANES, SUBLANES)], sem.at[buf_slot])

    def issue_all(p_ref, buf_slot):
        def issue(tt, carry):
            for kk in range(2):
                row_copy(p_ref, buf_slot, tt, kk).start()
            return carry

        lax.fori_loop(0, tc, issue, 0, unroll=4)

    @pl.when(i == 0)
    def _():
        issue_all(pos_ref, 0)

    @pl.when(i + 1 < pl.num_programs(0))
    def _():
        issue_all(posn_ref, 1 - cur)

    def drain(tt, carry):
        for kk in range(2):
            row_copy(pos_ref, cur, tt, kk).wait()
        return carry

    lax.fori_loop(0, tc, drain, 0, unroll=4)

    route = route_ref[...]
    w1 = route[:, 2:3]
    w2 = route[:, 3:4]
    for c in range(d // LANES):
        y1 = buf[cur, 0, pl.ds(c, tc, stride=SUBLANES), :]
        y2 = buf[cur, 1, pl.ds(c, tc, stride=SUBLANES), :]
        o_ref[:, c * LANES:(c + 1) * LANES] = x_ref[:, c * LANES:(c + 1) * LANES] + (w1 * y1 + w2 * y2)


def _combine(pos2, x2, route, y, *, tc):
    n, d = x2.shape
    last = n // tc - 1
    return pl.pallas_call(
        _combine_kernel,
        grid=(n // tc,),
        in_specs=[pl.BlockSpec((1, 1, 2 * tc), lambda i: (i, 0, 0), memory_space=pltpu.SMEM),
                  pl.BlockSpec((1, 1, 2 * tc), lambda i: (jnp.minimum(i + 1, last), 0, 0), memory_space=pltpu.SMEM),
                  pl.BlockSpec((tc, d), lambda i: (i, 0)),
                  pl.BlockSpec((tc, LANES), lambda i: (i, 0)),
                  pl.BlockSpec(memory_space=pl.ANY)],
        out_specs=pl.BlockSpec((tc, d), lambda i: (i, 0)),
        out_shape=jax.ShapeDtypeStruct((n, d), F32),
        scratch_shapes=[pltpu.VMEM((2, 2, tc * SUBLANES, LANES), F32), pltpu.SemaphoreType.DMA((2,))],
        compiler_params=_params("arbitrary"),
        name="moe_combine",
    )(pos2, pos2, x2, route, y)


def _tiles(b, s):
    n = b * s
    pick = lambda full, unit: full if unit % full == 0 else unit
    return dict(
        tm_in=pick(512, s), tm_loc=pick(512, s), tq=pick(1024, s), tm_merge=pick(512, n),
        tm_ffn=pick(1024, n), fc_dense=2816, t_moe=pick(1024, n) if n >= 8192 else 256, fc_moe=1792,
        td=pick(512, n), tc=pick(256, n),
    )


def _head_layout_constants():
    pq = np.zeros((ATTN_WIDTH + LANES, ATTN_HEADS * LANES), np.float32)
    pk = np.zeros_like(pq)
    oq = np.zeros((1, ATTN_HEADS * LANES), np.float32)
    ok = np.zeros_like(oq)
    ov = np.zeros_like(oq)
    for h in range(ATTN_HEADS):
        base = h * LANES
        for j in range(HEAD_DIM):
            pq[h * HEAD_DIM + j, base + j] = 1.0
            pk[h * HEAD_DIM + j, base + j] = 1.0
        for part in range(3):
            pq[ATTN_WIDTH + part * ATTN_HEADS + h, base + HEAD_DIM + part] = 1.0
            pk[ATTN_WIDTH + part * ATTN_HEADS + h, base + HEAD_DIM + 3 + part] = -1.0
            oq[0, base + HEAD_DIM + 3 + part] = 1.0
            ok[0, base + HEAD_DIM + part] = 1.0
        ov[0, base + HEAD_DIM:base + LANES] = 1.0
    return (jnp.asarray(pq, BF16), jnp.asarray(pk, BF16), jnp.asarray(oq), jnp.asarray(ok), jnp.asarray(ov))


def _block_diag(blocks):
    g, r, c = blocks.shape
    eye = jnp.eye(g, dtype=blocks.dtype)
    return (eye[:, None, :, None] * blocks[:, :, None, :]).reshape(g * r, g * c)


def _forward(x, g_mix, w_in, b_gate, b_forget, pool_w, pool_scale, conv_w, conv_b, conv_ln_g, conv_ln_b,
             q_norm_g, k_norm_g, w_br_pool, w_br_conv, w_br_attn, w_out, g_ffn, ffn_w1, ffn_w3, ffn_w2,
             router_w, exp_w1, exp_w3, exp_w2, tiles):
    b, s, d = x.shape
    n = b * s
    depth = w_in.shape[0]
    t = tiles

    off_conv = POOL_WIDTH
    off_q = off_conv + 2 * CONV_WIDTH
    off_f = off_q + 3 * ATTN_WIDTH
    off_g = off_f + ATTN_HEADS

    wlane = jnp.repeat(jnp.asarray(POOL_WINDOWS, F32), POOL_GROUP_DIM)[None, :]
    bd = _block_diag(jnp.full((ATTN_HEADS, HEAD_DIM, HEAD_DIM), 1.0 / HEAD_DIM, F32)).astype(BF16)
    tri = (lax.broadcasted_iota(jnp.int32, (t["tm_merge"], t["tm_merge"]), 1)
           < lax.broadcasted_iota(jnp.int32, (t["tm_merge"], t["tm_merge"]), 0)).astype(BF16)
    ffn_w1b, ffn_w3b, ffn_w2b = ffn_w1.astype(BF16), ffn_w3.astype(BF16), ffn_w2.astype(BF16)
    exp_w1b, exp_w3b, exp_w2b = exp_w1.astype(BF16), exp_w3.astype(BF16), exp_w2.astype(BF16)

    t_moe = t["t_moe"]
    n_tiles = (2 * n) // t_moe + N_EXPERTS
    n_slots = n_tiles * t_moe

    pq, pk, oq, ok, ov = _head_layout_constants()

    def head_pad(w, axis):
        shp = w.shape
        w = w.reshape(shp[:axis] + (ATTN_HEADS, HEAD_DIM) + shp[axis + 1:])
        pad = [(0, 0)] * w.ndim
        pad[axis + 1] = (0, LANES - HEAD_DIM)
        return jnp.pad(w, pad).reshape(shp[:axis] + (ATTN_HEADS * LANES,) + shp[axis + 1:])

    for l in range(depth):
        wl = w_in[l]
        wloc = wl[:, :off_q].astype(BF16)
        wqk = wl[:, off_q:off_q + 2 * ATTN_WIDTH].astype(BF16)
        wv = head_pad(wl[:, off_q + 2 * ATTN_WIDTH:off_f], 1).astype(BF16)
        wf3 = jnp.pad(jnp.tile(wl[:, off_f:off_g], (1, 3)), ((0, 0), (0, LANES - 3 * ATTN_HEADS))).astype(BF16)
        bf3 = jnp.pad(jnp.tile(b_forget[l], 3), (0, LANES - 3 * ATTN_HEADS))[None, :]
        wg = wl[:, off_g:].astype(BF16)
        qg = jnp.tile(q_norm_g[l], ATTN_HEADS)[None, :] * (HEAD_DIM ** -0.5 * LOG2E)
        kg = jnp.tile(k_norm_g[l], ATTN_HEADS)[None, :]
        pool_in, z, q, k, v, gates = _inproj(
            x, g_mix[l][None, :], wloc, wqk, wv, wf3, wg, b_gate[l][None, :], bf3, qg, kg, bd, pq, pk, oq, ok, ov,
            tm=t["tm_in"])

        yp, yc = _local(pool_in, z, _block_diag(pool_w[l]).astype(BF16), pool_scale[l][None, :], wlane,
                        conv_w[l], conv_b[l][None, :], conv_ln_g[l][None, :], conv_ln_b[l][None, :],
                        tm=t["tm_loc"])

        ya = _attention(q, k, v, tq=t["tq"])

        merge_args = (x.reshape(n, d), yp.reshape(n, -1), yc.reshape(n, -1), ya.reshape(n, -1),
                      gates.reshape(n, -1), w_br_pool[l].astype(BF16), w_br_conv[l].astype(BF16),
                      head_pad(w_br_attn[l], 0).astype(BF16), w_out[l].astype(BF16), g_ffn[l][None, :])
        i = l // 2
        if l % 2 == 0:
            xn, h2 = _merge(*merge_args, tm=t["tm_merge"])
            x2 = _ffn_dense(xn, h2, ffn_w1b[i:i + 1], ffn_w3b[i:i + 1], ffn_w2b[i:i + 1],
                            tm=t["tm_ffn"], fc=t["fc_dense"])
        else:
            wr32 = jnp.pad(router_w[i], ((0, 0), (0, LANES - N_EXPERTS)))
            wr_hi = wr32.astype(BF16)
            wr = jnp.concatenate([wr_hi, (wr32 - wr_hi.astype(F32)).astype(BF16)], axis=1)
            xn, h2t, route, counts = _merge(*merge_args, wr, tri, tm=t["tm_merge"])
            cnt = counts[0, :N_EXPERTS].astype(jnp.int32)
            ntile_e = (cnt + t_moe - 1) // t_moe
            ends = jnp.cumsum(ntile_e) * t_moe
            starts = ends - ntile_e * t_moe
            idx = route[:, 0:2].astype(jnp.int32)
            rank = route[:, 4:6].astype(jnp.int32)
            pos = starts[idx] + rank
            tile_first = jnp.arange(n_tiles, dtype=jnp.int32) * t_moe
            tile_expert = jnp.minimum(jnp.sum(tile_first[:, None] >= ends[None, :], axis=1),
                                      N_EXPERTS - 1).astype(jnp.int32)
            n_used = (ends[-1:] // t_moe).astype(jnp.int32)
            xs = _dispatch(pos.reshape(n // t["td"], 1, 2 * t["td"]), h2t, n_slots, td=t["td"])
            y = _ffn_grouped(tile_expert, n_used, xs, exp_w1b[i], exp_w3b[i], exp_w2b[i], t=t_moe, fc=t["fc_moe"])
            x2 = _combine(pos.reshape(n // t["tc"], 1, 2 * t["tc"]), xn, route, y, tc=t["tc"])
        x = x2.reshape(b, s, d)
    return x


def kernel(x, g_mix, w_in, b_gate, b_forget, pool_w, pool_scale, conv_w, conv_b, conv_ln_g, conv_ln_b,
           q_norm_g, k_norm_g, w_br_pool, w_br_conv, w_br_attn, w_out, g_ffn, ffn_w1, ffn_w3, ffn_w2,
           router_w, exp_w1, exp_w3, exp_w2):
    return _forward(x, g_mix, w_in, b_gate, b_forget, pool_w, pool_scale, conv_w, conv_b, conv_ln_g, conv_ln_b,
                    q_norm_g, k_norm_g, w_br_pool, w_br_conv, w_br_attn, w_out, g_ffn, ffn_w1, ffn_w3, ffn_w2,
                    router_w, exp_w1, exp_w3, exp_w2, _tiles(x.shape[0], x.shape[1]))
```

```python
import functools

import jax
import jax.numpy as jnp
import numpy as np
from jax import lax
from jax.experimental import pallas as pl
from jax.experimental.pallas import tpu as pltpu

F32 = jnp.float32
BF16 = jnp.bfloat16

EPS = 1e-6
LOG2E = 1.4426950408889634
POOL_WINDOWS = (2, 4, 8, 16)
POOL_WIDTH = 256
POOL_GROUP_DIM = 64
CONV_WIDTH = 256
CONV_KERNEL = 31
ATTN_HEADS = 8
HEAD_DIM = 64
ATTN_WIDTH = ATTN_HEADS * HEAD_DIM
N_BRANCH = 3
N_EXPERTS = 8
LANES = 128
SUBLANES = 8
HALO = 32
VMEM_LIMIT = 56 * 1024 * 1024


def _params(*sem):
    return pltpu.CompilerParams(dimension_semantics=sem, vmem_limit_bytes=VMEM_LIMIT)


def _const_spec(shape):
    nd = len(shape)
    return pl.BlockSpec(shape, lambda *_: (0,) * nd, pipeline_mode=pl.Buffered(1))


def _local_mixers(p, z, si, wpd_ref, pscale_ref, wlane_ref, cw_ref, cb_ref, lng_ref, lnb_ref, pbuf, zbuf):
    tm = p.shape[0]
    pbuf[HALO:HALO + tm, :] = p
    zbuf[HALO:HALO + tm, :] = z

    wlane = wlane_ref[...]
    acc = p
    pooled_sum = jnp.zeros_like(p)
    for j in range(1, max(POOL_WINDOWS)):
        acc = acc + pbuf[HALO - j:HALO - j + tm, :]
        if (j + 1) in POOL_WINDOWS:
            pooled_sum = jnp.where(wlane == float(j + 1), acc, pooled_sum)
    t1 = (lax.broadcasted_iota(jnp.int32, (tm, 1), 0) + si * tm + 1).astype(F32)
    count = jnp.minimum(t1, wlane)
    pooled = pooled_sum / count - p
    yp = jnp.dot(pooled.astype(BF16), wpd_ref[...], preferred_element_type=F32) * pscale_ref[...]

    conv = jnp.zeros((tm, CONV_WIDTH), F32) + cb_ref[...]
    for kk in range(CONV_KERNEL):
        off = HALO - (CONV_KERNEL - 1) + kk
        conv = conv + zbuf[off:off + tm, :] * cw_ref[kk:kk + 1, :]
    mu = jnp.mean(conv, axis=-1, keepdims=True)
    cen = conv - mu
    var = jnp.mean(cen * cen, axis=-1, keepdims=True)
    y = cen * lax.rsqrt(var + EPS) * lng_ref[...] + lnb_ref[...]

    pbuf[0:HALO, :] = pbuf[tm:tm + HALO, :]
    zbuf[0:HALO, :] = zbuf[tm:tm + HALO, :]
    return yp.astype(BF16), (y * jax.nn.sigmoid(y)).astype(BF16)


def _inproj_kernel(x_ref, g_ref, wloc_ref, wqk_ref, wv_ref, wf_ref, wg_ref, bg_ref, bf_ref, qg_ref, kg_ref, bd_ref,
                   pq_ref, pk_ref, oq_ref, ok_ref, ov_ref,
                   wpd_ref, pscale_ref, wlane_ref, cw_ref, cb_ref, lng_ref, lnb_ref,
                   yp_ref, yc_ref, q_ref, k_ref, v_ref, gates_ref, carry_ref, pbuf, zbuf):
    tm = x_ref.shape[1]
    si = pl.program_id(1)

    @pl.when(si == 0)
    def _():
        carry_ref[...] = jnp.zeros_like(carry_ref)
        pbuf[0:HALO, :] = jnp.zeros((HALO, POOL_WIDTH), F32)
        zbuf[0:HALO, :] = jnp.zeros((HALO, CONV_WIDTH), F32)

    x = x_ref[0]
    ms = jnp.mean(x * x, axis=-1, keepdims=True)
    h = (x * lax.rsqrt(ms + EPS) * g_ref[...]).astype(BF16)

    loc = jnp.dot(h, wloc_ref[...], preferred_element_type=F32)
    z = loc[:, POOL_WIDTH:POOL_WIDTH + CONV_WIDTH] * jax.nn.sigmoid(loc[:, POOL_WIDTH + CONV_WIDTH:])
    yp_ref[0], yc_ref[0] = _local_mixers(loc[:, :POOL_WIDTH], z, si, wpd_ref, pscale_ref, wlane_ref, cw_ref,
                                         cb_ref, lng_ref, lnb_ref, pbuf, zbuf)

    fl = jnp.dot(h, wf_ref[...], preferred_element_type=F32) + bf_ref[...]
    cs = jnp.minimum(fl, 0.0) - jnp.log1p(jnp.exp(-jnp.abs(fl)))
    row = lax.broadcasted_iota(jnp.int32, cs.shape, 0)
    shift = 1
    while shift < tm:
        cs = cs + jnp.where(row >= shift, pltpu.roll(cs, shift, axis=0), 0.0)
        shift *= 2
    cs = cs + carry_ref[0:1, :]
    carry_ref[0:1, :] = cs[tm - 1:tm, :]
    c2 = cs * LOG2E
    hi = c2.astype(BF16)
    rem = c2 - hi.astype(F32)
    mid = rem.astype(BF16)
    lo = (rem - mid.astype(F32)).astype(BF16)
    lane = lax.broadcasted_iota(jnp.int32, cs.shape, 1)
    cpack = jnp.where(lane < ATTN_HEADS, hi, jnp.where(lane < 2 * ATTN_HEADS, mid, lo))

    qk = jnp.dot(h, wqk_ref[...], preferred_element_type=F32)
    q = qk[:, :ATTN_WIDTH]
    k = qk[:, ATTN_WIDTH:]
    qss = jnp.dot((q * q).astype(BF16), bd_ref[...], preferred_element_type=F32)
    kss = jnp.dot((k * k).astype(BF16), bd_ref[...], preferred_element_type=F32)
    qn = (q * lax.rsqrt(qss + EPS) * qg_ref[...]).astype(BF16)
    kn = (k * lax.rsqrt(kss + EPS) * kg_ref[...]).astype(BF16)
    q_ref[0] = (jnp.dot(jnp.concatenate([qn, cpack], axis=1), pq_ref[...], preferred_element_type=F32)
                + oq_ref[...]).astype(BF16)
    k_ref[0] = (jnp.dot(jnp.concatenate([kn, cpack], axis=1), pk_ref[...], preferred_element_type=F32)
                + ok_ref[...]).astype(BF16)
    v_ref[0] = (jnp.dot(h, wv_ref[...], preferred_element_type=F32) + ov_ref[...]).T.astype(BF16)

    gl = jnp.dot(h, wg_ref[...], preferred_element_type=F32) + bg_ref[...]
    gates_ref[0] = jax.nn.sigmoid(gl).astype(BF16)


def _inproj(x, *consts, tm):
    b, s, d = x.shape
    grid = (b, s // tm)
    row = lambda w: pl.BlockSpec((1, tm, w), lambda bi, si: (bi, si, 0))
    head_w = ATTN_HEADS * LANES
    out_shape = (
        jax.ShapeDtypeStruct((b, s, POOL_WIDTH), BF16),
        jax.ShapeDtypeStruct((b, s, CONV_WIDTH), BF16),
        jax.ShapeDtypeStruct((b, s, head_w), BF16),
        jax.ShapeDtypeStruct((b, s, head_w), BF16),
        jax.ShapeDtypeStruct((b, head_w, s), BF16),
        jax.ShapeDtypeStruct((b, s, N_BRANCH * d), BF16),
    )
    vt_spec = pl.BlockSpec((1, head_w, tm), lambda bi, si: (bi, 0, si))
    return pl.pallas_call(
        _inproj_kernel,
        grid=grid,
        in_specs=[row(d)] + [_const_spec(a.shape) for a in consts],
        out_specs=(row(POOL_WIDTH), row(CONV_WIDTH), row(head_w), row(head_w), vt_spec, row(N_BRANCH * d)),
        out_shape=out_shape,
        scratch_shapes=[pltpu.VMEM((SUBLANES, LANES), F32), pltpu.VMEM((HALO + tm, POOL_WIDTH), F32),
                        pltpu.VMEM((HALO + tm, CONV_WIDTH), F32)],
        compiler_params=_params("arbitrary", "arbitrary"),
        name="inproj",
    )(x, *consts)


def _attn_kernel(q_ref, k_ref, v_ref, o_ref, m_ref, acc_ref, s_ref, tmax_ref):
    tq = q_ref.shape[1]
    tk = s_ref.shape[1]
    qi = pl.program_id(2)
    q = q_ref[0]
    m_ref[...] = jnp.full(m_ref.shape, -jnp.inf, F32)
    acc_ref[...] = jnp.zeros(acc_ref.shape, F32)

    def scores(j, slot, diag_offset=None):
        koff = pl.multiple_of(j * tk, tk)
        s = lax.dot_general(k_ref[0, pl.ds(koff, tk), :], q, (((1,), (1,)), ((), ())),
                            preferred_element_type=F32)
        if diag_offset is not None:
            kpos = lax.broadcasted_iota(jnp.int32, (tk, tq), 0) + diag_offset
            s = jnp.where(kpos <= lax.broadcasted_iota(jnp.int32, (tk, tq), 1), s, -jnp.inf)
        s_ref[slot] = s
        tmax_ref[slot] = jnp.max(s, axis=0, keepdims=True)

    def consume(j, slot):
        koff = pl.multiple_of(j * tk, tk)
        m_old = m_ref[...]
        m_new = jnp.maximum(m_old, tmax_ref[slot])
        p = jnp.exp2((s_ref[slot] - m_new).astype(BF16))
        acc_ref[...] = jnp.exp2(m_old - m_new) * acc_ref[...] + jnp.dot(v_ref[0, :, pl.ds(koff, tk)], p,
                                                                        preferred_element_type=F32)
        m_ref[...] = m_new

    def pair(i, carry):
        scores(2 * i + 1, 1)
        consume(2 * i, 0)
        scores(2 * i + 2, 0)
        consume(2 * i + 1, 1)
        return carry

    @pl.when(qi == 0)
    def _():
        scores(0, 0, diag_offset=0)

    @pl.when(qi > 0)
    def _():
        scores(0, 0)
        lax.fori_loop(0, qi - 1, pair, 0)
        scores(2 * qi - 1, 1)
        consume(2 * qi - 2, 0)
        scores(2 * qi, 0, diag_offset=0)
        consume(2 * qi - 1, 1)

    scores(2 * qi + 1, 1, diag_offset=tk)
    consume(2 * qi, 0)
    consume(2 * qi + 1, 1)

    acc = acc_ref[...]
    out_t = acc[:HEAD_DIM, :] / acc[HEAD_DIM:HEAD_DIM + 1, :]
    o_ref[0] = jnp.concatenate([out_t, jnp.zeros_like(out_t)], axis=0).T.astype(BF16)


def _attention(q, k, v, *, tq):
    b, s, _ = q.shape
    tk = tq // 2
    return pl.pallas_call(
        _attn_kernel,
        grid=(b, ATTN_HEADS, s // tq),
        in_specs=[
            pl.BlockSpec((1, tq, LANES), lambda bi, hi, qi: (bi, qi, hi)),
            pl.BlockSpec((1, s, LANES), lambda bi, hi, qi: (bi, 0, hi)),
            pl.BlockSpec((1, LANES, s), lambda bi, hi, qi: (bi, hi, 0)),
        ],
        out_specs=pl.BlockSpec((1, tq, LANES), lambda bi, hi, qi: (bi, qi, hi)),
        out_shape=jax.ShapeDtypeStruct((b, s, ATTN_HEADS * LANES), BF16),
        scratch_shapes=[pltpu.VMEM((1, tq), F32), pltpu.VMEM((LANES, tq), F32),
                        pltpu.VMEM((2, tk, tq), F32), pltpu.VMEM((2, 1, tq), F32)],
        compiler_params=_params("arbitrary", "arbitrary", "arbitrary"),
        name="fox_attention",
    )(q, k, v)


def _merge_core(x_ref, yp_ref, yc_ref, ya_ref, gates_ref, wp_ref, wc_ref, wa_ref, wo_ref, gf_ref):
    d = x_ref.shape[1]
    gates = gates_ref[...]
    merged = (gates[:, :d].astype(F32) * jnp.dot(yp_ref[...], wp_ref[...], preferred_element_type=F32)
              + gates[:, d:2 * d].astype(F32) * jnp.dot(yc_ref[...], wc_ref[...], preferred_element_type=F32)
              + gates[:, 2 * d:].astype(F32) * jnp.dot(ya_ref[...], wa_ref[...], preferred_element_type=F32))
    xn = x_ref[...] + jnp.dot(merged.astype(BF16), wo_ref[...], preferred_element_type=F32)
    ms = jnp.mean(xn * xn, axis=-1, keepdims=True)
    h2 = xn * lax.rsqrt(ms + EPS) * gf_ref[...]
    return xn, h2


def _merge_dense_kernel(x_ref, yp_ref, yc_ref, ya_ref, gates_ref, wp_ref, wc_ref, wa_ref, wo_ref, gf_ref,
                        xn_ref, h2_ref):
    xn, h2 = _merge_core(x_ref, yp_ref, yc_ref, ya_ref, gates_ref, wp_ref, wc_ref, wa_ref, wo_ref, gf_ref)
    xn_ref[...] = xn
    h2_ref[...] = h2.astype(BF16)


def _merge_moe_kernel(x_ref, yp_ref, yc_ref, ya_ref, gates_ref, wp_ref, wc_ref, wa_ref, wo_ref, gf_ref,
                      wr_ref, tri_ref, xn_ref, h2t_ref, route_ref, counts_ref, carry_ref):
    tm, d = x_ref.shape

    @pl.when(pl.program_id(0) == 0)
    def _():
        carry_ref[...] = jnp.zeros_like(carry_ref)

    xn, h2 = _merge_core(x_ref, yp_ref, yc_ref, ya_ref, gates_ref, wp_ref, wc_ref, wa_ref, wo_ref, gf_ref)
    xn_ref[...] = xn
    for c in range(d // LANES):
        h2t_ref[pl.ds(c, tm, stride=SUBLANES), :] = h2[:, c * LANES:(c + 1) * LANES]

    h_hi = h2.astype(BF16)
    h_lo = (h2 - h_hi.astype(F32)).astype(BF16)
    both = jnp.dot(h_hi, wr_ref[...], preferred_element_type=F32)
    logits = (both[:, :LANES] + both[:, LANES:]
              + jnp.dot(h_lo, wr_ref[:, :LANES], preferred_element_type=F32))
    lane = lax.broadcasted_iota(jnp.int32, (tm, LANES), 1)
    lanef = lane.astype(F32)
    lg = jnp.where(lane < N_EXPERTS, logits, -jnp.inf)
    m1 = jnp.max(lg, axis=-1, keepdims=True)
    i1 = jnp.min(jnp.where(lg == m1, lanef, float(LANES)), axis=-1, keepdims=True)
    lg2 = jnp.where(lanef == i1, -jnp.inf, lg)
    m2 = jnp.max(lg2, axis=-1, keepdims=True)
    i2 = jnp.min(jnp.where(lg2 == m2, lanef, float(LANES)), axis=-1, keepdims=True)
    e2 = jnp.exp(m2 - m1)
    w1 = 1.0 / (1.0 + e2)
    w2 = e2 / (1.0 + e2)

    is1 = lanef == i1
    is2 = lanef == i2
    sel = jnp.where(is1 | is2, 1.0, 0.0)
    rank = jnp.dot(tri_ref[...], sel.astype(BF16), preferred_element_type=F32) + carry_ref[0:1, :]
    r1 = jnp.sum(jnp.where(is1, rank, 0.0), axis=-1, keepdims=True)
    r2 = jnp.sum(jnp.where(is2, rank, 0.0), axis=-1, keepdims=True)
    total = rank[tm - 1:tm, :] + sel[tm - 1:tm, :]
    carry_ref[0:1, :] = total
    counts_ref[...] = jnp.broadcast_to(total, counts_ref.shape)

    route = jnp.zeros((tm, LANES), F32)
    for li, val in enumerate((i1, i2, w1, w2, r1, r2)):
        route = jnp.where(lane == li, val, route)
    route_ref[...] = route


def _merge(x2, yp, yc, ya, gates, wp, wc, wa, wo, gf, wr=None, tri=None, *, tm):
    n, d = x2.shape
    row = lambda w: pl.BlockSpec((tm, w), lambda i: (i, 0))
    ins = [x2, yp, yc, ya, gates, wp, wc, wa, wo, gf]
    in_specs = [row(d), row(POOL_WIDTH), row(CONV_WIDTH), row(ya.shape[1]), row(N_BRANCH * d)] + [
        _const_spec(a.shape) for a in (wp, wc, wa, wo, gf)]
    if wr is None:
        return pl.pallas_call(
            _merge_dense_kernel,
            grid=(n // tm,),
            in_specs=in_specs,
            out_specs=(row(d), row(d)),
            out_shape=(jax.ShapeDtypeStruct((n, d), F32), jax.ShapeDtypeStruct((n, d), BF16)),
            compiler_params=_params("arbitrary"),
            name="merge_dense",
        )(*ins)
    return pl.pallas_call(
        _merge_moe_kernel,
        grid=(n // tm,),
        in_specs=in_specs + [_const_spec(wr.shape), _const_spec(tri.shape)],
        out_specs=(row(d), pl.BlockSpec((tm * SUBLANES, LANES), lambda i: (i, 0)), row(LANES),
                   pl.BlockSpec((SUBLANES, LANES), lambda i: (0, 0))),
        out_shape=(jax.ShapeDtypeStruct((n, d), F32), jax.ShapeDtypeStruct((n * SUBLANES, LANES), F32),
                   jax.ShapeDtypeStruct((n, LANES), F32), jax.ShapeDtypeStruct((SUBLANES, LANES), F32)),
        scratch_shapes=[pltpu.VMEM((SUBLANES, LANES), F32)],
        compiler_params=_params("arbitrary"),
        name="merge_moe",
    )(*ins, wr, tri)


FF_SUB = 256


def _swiglu_accumulate(xb, w1_ref, w3_ref, w2_ref, acc_ref):
    fc = w1_ref.shape[2]
    for c in range(fc // FF_SUB):
        cols = slice(c * FF_SUB, (c + 1) * FF_SUB)
        a = jnp.dot(xb, w1_ref[0, :, cols], preferred_element_type=F32)
        b = jnp.dot(xb, w3_ref[0, :, cols], preferred_element_type=F32)
        hmid = (a * jax.nn.sigmoid(a) * b).astype(BF16)
        acc_ref[...] += jnp.dot(hmid, w2_ref[0, cols, :], preferred_element_type=F32)


def _ffn_dense_kernel(x_ref, h_ref, w1_ref, w3_ref, w2_ref, o_ref, acc_ref):
    f = pl.program_id(1)

    @pl.when(f == 0)
    def _():
        acc_ref[...] = x_ref[...]

    _swiglu_accumulate(h_ref[...], w1_ref, w3_ref, w2_ref, acc_ref)

    @pl.when(f == pl.num_programs(1) - 1)
    def _():
        o_ref[...] = acc_ref[...]


def _ffn_dense(x2, h2, w1, w3, w2, *, tm, fc):
    n, d = x2.shape
    ff = w1.shape[2]
    row = lambda: pl.BlockSpec((tm, d), lambda i, f: (i, 0))
    once = dict(pipeline_mode=pl.Buffered(1)) if fc == ff else {}
    return pl.pallas_call(
        _ffn_dense_kernel,
        grid=(n // tm, ff // fc),
        in_specs=[row(), row(),
                  pl.BlockSpec((1, d, fc), lambda i, f: (0, 0, f), **once),
                  pl.BlockSpec((1, d, fc), lambda i, f: (0, 0, f), **once),
                  pl.BlockSpec((1, fc, d), lambda i, f: (0, f, 0), **once)],
        out_specs=row(),
        out_shape=jax.ShapeDtypeStruct((n, d), F32),
        scratch_shapes=[pltpu.VMEM((tm, d), F32)],
        compiler_params=_params("arbitrary", "arbitrary"),
        name="ffn_dense",
    )(x2, h2, w1, w3, w2)


def _ffn_grouped_kernel(te_ref, nused_ref, xs_ref, w1_ref, w3_ref, w2_ref, y_ref, xb_ref, acc_ref):
    t, d = xb_ref.shape
    i = pl.program_id(0)
    f = pl.program_id(1)
    live = i < nused_ref[0]

    @pl.when(f == 0)
    def _():
        for c in range(d // LANES):
            xb_ref[:, c * LANES:(c + 1) * LANES] = xs_ref[pl.ds(c, t, stride=SUBLANES), :].astype(BF16)
        acc_ref[...] = jnp.zeros_like(acc_ref)

    @pl.when(live)
    def _():
        _swiglu_accumulate(xb_ref[...], w1_ref, w3_ref, w2_ref, acc_ref)

    @pl.when(f == pl.num_programs(1) - 1)
    def _():
        for c in range(d // LANES):
            y_ref[pl.ds(c, t, stride=SUBLANES), :] = acc_ref[:, c * LANES:(c + 1) * LANES]


def _ffn_grouped(tile_expert, n_used, xs, w1, w3, w2, *, t, fc):
    rows, _ = xs.shape
    _, d, ff = w1.shape
    n_tiles = rows // (t * SUBLANES)
    blk = pl.BlockSpec((t * SUBLANES, LANES), lambda i, f, te, nu: (i, 0))
    grid_spec = pltpu.PrefetchScalarGridSpec(
        num_scalar_prefetch=2,
        grid=(n_tiles, ff // fc),
        in_specs=[blk,
                  pl.BlockSpec((1, d, fc), lambda i, f, te, nu: (te[i], 0, f)),
                  pl.BlockSpec((1, d, fc), lambda i, f, te, nu: (te[i], 0, f)),
                  pl.BlockSpec((1, fc, d), lambda i, f, te, nu: (te[i], f, 0))],
        out_specs=blk,
        scratch_shapes=[pltpu.VMEM((t, d), BF16), pltpu.VMEM((t, d), F32)],
    )
    return pl.pallas_call(
        _ffn_grouped_kernel,
        grid_spec=grid_spec,
        out_shape=jax.ShapeDtypeStruct(xs.shape, F32),
        compiler_params=_params("arbitrary", "arbitrary"),
        name="ffn_grouped",
    )(tile_expert, n_used, xs, w1, w3, w2)


def _dispatch_kernel(pos_ref, h_ref, xs_in, xs_hbm, sem):
    del xs_in
    td = pos_ref.shape[2] // 2

    def row_copy(tt, kk):
        slot = pos_ref[0, 0, 2 * tt + kk]
        return pltpu.make_async_copy(h_ref.at[pl.ds(tt * SUBLANES, SUBLANES)],
                                     xs_hbm.at[pl.ds(slot * SUBLANES, SUBLANES)], sem)

    def issue(tt, carry):
        for kk in range(2):
            row_copy(tt, kk).start()
        return carry

    lax.fori_loop(0, td, issue, 0, unroll=4)

    def drain(tt, carry):
        for kk in range(2):
            row_copy(tt, kk).wait()
        return carry

    lax.fori_loop(0, td, drain, 0, unroll=4)


def _dispatch(pos2, h2t, n_slots, *, td):
    n = h2t.shape[0] // SUBLANES
    xs0 = jnp.zeros((n_slots * SUBLANES, LANES), F32)
    return pl.pallas_call(
        _dispatch_kernel,
        grid=(n // td,),
        in_specs=[pl.BlockSpec((1, 1, 2 * td), lambda i: (i, 0, 0), memory_space=pltpu.SMEM),
                  pl.BlockSpec((td * SUBLANES, LANES), lambda i: (i, 0)),
                  pl.BlockSpec(memory_space=pl.ANY)],
        out_specs=pl.BlockSpec(memory_space=pl.ANY),
        out_shape=jax.ShapeDtypeStruct(xs0.shape, F32),
        scratch_shapes=[pltpu.SemaphoreType.DMA(())],
        input_output_aliases={2: 0},
        compiler_params=_params("arbitrary"),
        name="moe_dispatch",
    )(pos2, h2t, xs0)


def _combine_kernel(pos_ref, posn_ref, x_ref, route_ref, y_hbm, o_ref, buf, sem):
    tc, d = x_ref.shape
    i = pl.program_id(0)
    cur = i % 2

    def row_copy(p_ref, buf_slot, tt, kk):
        slot = p_ref[0, 0, 2 * tt + kk]
        return pltpu.make_async_copy(y_hbm.at[pl.ds(slot * SUBLANES, SUBLANES)],
                                     buf.at[buf_slot, kk, pl.ds(tt * SUBLANES, SUBLANES)], sem.at[buf_slot])

    def issue_all(p_ref, buf_slot):
        def issue(tt, carry):
            for kk in range(2):
                row_copy(p_ref, buf_slot, tt, kk).start()
            return carry

        lax.fori_loop(0, tc, issue, 0, unroll=4)

    @pl.when(i == 0)
    def _():
        issue_all(pos_ref, 0)

    @pl.when(i + 1 < pl.num_programs(0))
    def _():
        issue_all(posn_ref, 1 - cur)

    def drain(tt, carry):
        for kk in range(2):
            row_copy(pos_ref, cur, tt, kk).wait()
        return carry

    lax.fori_loop(0, tc, drain, 0, unroll=4)

    route = route_ref[...]
    w1 = route[:, 2:3]
    w2 = route[:, 3:4]
    for c in range(d // LANES):
        y1 = buf[cur, 0, pl.ds(c, tc, stride=SUBLANES), :]
        y2 = buf[cur, 1, pl.ds(c, tc, stride=SUBLANES), :]
        o_ref[:, c * LANES:(c + 1) * LANES] = x_ref[:, c * LANES:(c + 1) * LANES] + (w1 * y1 + w2 * y2)


def _combine(pos2, x2, route, y, *, tc):
    n, d = x2.shape
    last = n // tc - 1
    return pl.pallas_call(
        _combine_kernel,
        grid=(n // tc,),
        in_specs=[pl.BlockSpec((1, 1, 2 * tc), lambda i: (i, 0, 0), memory_space=pltpu.SMEM),
                  pl.BlockSpec((1, 1, 2 * tc), lambda i: (jnp.minimum(i + 1, last), 0, 0), memory_space=pltpu.SMEM),
                  pl.BlockSpec((tc, d), lambda i: (i, 0)),
                  pl.BlockSpec((tc, LANES), lambda i: (i, 0)),
                  pl.BlockSpec(memory_space=pl.ANY)],
        out_specs=pl.BlockSpec((tc, d), lambda i: (i, 0)),
        out_shape=jax.ShapeDtypeStruct((n, d), F32),
        scratch_shapes=[pltpu.VMEM((2, 2, tc * SUBLANES, LANES), F32), pltpu.SemaphoreType.DMA((2,))],
        compiler_params=_params("arbitrary"),
        name="moe_combine",
    )(pos2, pos2, x2, route, y)


def _tiles(b, s):
    n = b * s
    pick = lambda full, unit: full if unit % full == 0 else unit
    return dict(
        tm_in=pick(512, s), tm_loc=pick(512, s), tq=pick(1024, s), tm_merge=pick(512, n),
        tm_ffn=pick(1024, n), fc_dense=2816, t_moe=pick(1024, n) if n >= 8192 else 256, fc_moe=1792,
        td=pick(512, n), tc=pick(256, n),
    )


def _head_layout_constants():
    pq = np.zeros((ATTN_WIDTH + LANES, ATTN_HEADS * LANES), np.float32)
    pk = np.zeros_like(pq)
    oq = np.zeros((1, ATTN_HEADS * LANES), np.float32)
    ok = np.zeros_like(oq)
    ov = np.zeros_like(oq)
    for h in range(ATTN_HEADS):
        base = h * LANES
        for j in range(HEAD_DIM):
            pq[h * HEAD_DIM + j, base + j] = 1.0
            pk[h * HEAD_DIM + j, base + j] = 1.0
        for part in range(3):
            pq[ATTN_WIDTH + part * ATTN_HEADS + h, base + HEAD_DIM + part] = 1.0
            pk[ATTN_WIDTH + part * ATTN_HEADS + h, base + HEAD_DIM + 3 + part] = -1.0
            oq[0, base + HEAD_DIM + 3 + part] = 1.0
            ok[0, base + HEAD_DIM + part] = 1.0
        ov[0, base + HEAD_DIM:base + LANES] = 1.0
    return (jnp.asarray(pq, BF16), jnp.asarray(pk, BF16), jnp.asarray(oq), jnp.asarray(ok), jnp.asarray(ov))


def _block_diag(blocks):
    g, r, c = blocks.shape
    eye = jnp.eye(g, dtype=blocks.dtype)
    return (eye[:, None, :, None] * blocks[:, :, None, :]).reshape(g * r, g * c)


def _forward(x, g_mix, w_in, b_gate, b_forget, pool_w, pool_scale, conv_w, conv_b, conv_ln_g, conv_ln_b,
             q_norm_g, k_norm_g, w_br_pool, w_br_conv, w_br_attn, w_out, g_ffn, ffn_w1, ffn_w3, ffn_w2,
             router_w, exp_w1, exp_w3, exp_w2, tiles):
    b, s, d = x.shape
    n = b * s
    depth = w_in.shape[0]
    t = tiles

    off_conv = POOL_WIDTH
    off_q = off_conv + 2 * CONV_WIDTH
    off_f = off_q + 3 * ATTN_WIDTH
    off_g = off_f + ATTN_HEADS

    wlane = jnp.repeat(jnp.asarray(POOL_WINDOWS, F32), POOL_GROUP_DIM)[None, :]
    bd = _block_diag(jnp.full((ATTN_HEADS, HEAD_DIM, HEAD_DIM), 1.0 / HEAD_DIM, F32)).astype(BF16)
    tri = (lax.broadcasted_iota(jnp.int32, (t["tm_merge"], t["tm_merge"]), 1)
           < lax.broadcasted_iota(jnp.int32, (t["tm_merge"], t["tm_merge"]), 0)).astype(BF16)
    ffn_w1b, ffn_w3b, ffn_w2b = ffn_w1.astype(BF16), ffn_w3.astype(BF16), ffn_w2.astype(BF16)
    exp_w1b, exp_w3b, exp_w2b = exp_w1.astype(BF16), exp_w3.astype(BF16), exp_w2.astype(BF16)

    t_moe = t["t_moe"]
    n_tiles = (2 * n) // t_moe + N_EXPERTS
    n_slots = n_tiles * t_moe

    pq, pk, oq, ok, ov = _head_layout_constants()

    def head_pad(w, axis):
        shp = w.shape
        w = w.reshape(shp[:axis] + (ATTN_HEADS, HEAD_DIM) + shp[axis + 1:])
        pad = [(0, 0)] * w.ndim
        pad[axis + 1] = (0, LANES - HEAD_DIM)
        return jnp.pad(w, pad).reshape(shp[:axis] + (ATTN_HEADS * LANES,) + shp[axis + 1:])

    for l in range(depth):
        wl = w_in[l]
        wloc = wl[:, :off_q].astype(BF16)
        wqk = wl[:, off_q:off_q + 2 * ATTN_WIDTH].astype(BF16)
        wv = head_pad(wl[:, off_q + 2 * ATTN_WIDTH:off_f], 1).astype(BF16)
        wf3 = jnp.pad(jnp.tile(wl[:, off_f:off_g], (1, 3)), ((0, 0), (0, LANES - 3 * ATTN_HEADS))).astype(BF16)
        bf3 = jnp.pad(jnp.tile(b_forget[l], 3), (0, LANES - 3 * ATTN_HEADS))[None, :]
        wg = wl[:, off_g:].astype(BF16)
        qg = jnp.tile(q_norm_g[l], ATTN_HEADS)[None, :] * (HEAD_DIM ** -0.5 * LOG2E)
        kg = jnp.tile(k_norm_g[l], ATTN_HEADS)[None, :]
        yp, yc, q, k, v, gates = _inproj(
            x, g_mix[l][None, :], wloc, wqk, wv, wf3, wg, b_gate[l][None, :], bf3, qg, kg, bd, pq, pk, oq, ok, ov,
            _block_diag(pool_w[l]).astype(BF16), pool_scale[l][None, :], wlane,
            conv_w[l], conv_b[l][None, :], conv_ln_g[l][None, :], conv_ln_b[l][None, :],
            tm=t["tm_in"])

        ya = _attention(q, k, v, tq=t["tq"])

        merge_args = (x.reshape(n, d), yp.reshape(n, -1), yc.reshape(n, -1), ya.reshape(n, -1),
                      gates.reshape(n, -1), w_br_pool[l].astype(BF16), w_br_conv[l].astype(BF16),
                      head_pad(w_br_attn[l], 0).astype(BF16), w_out[l].astype(BF16), g_ffn[l][None, :])
        i = l // 2
        if l % 2 == 0:
            xn, h2 = _merge(*merge_args, tm=t["tm_merge"])
            x2 = _ffn_dense(xn, h2, ffn_w1b[i:i + 1], ffn_w3b[i:i + 1], ffn_w2b[i:i + 1],
                            tm=t["tm_ffn"], fc=t["fc_dense"])
        else:
            wr32 = jnp.pad(router_w[i], ((0, 0), (0, LANES - N_EXPERTS)))
            wr_hi = wr32.astype(BF16)
            wr = jnp.concatenate([wr_hi, (wr32 - wr_hi.astype(F32)).astype(BF16)], axis=1)
            xn, h2t, route, counts = _merge(*merge_args, wr, tri, tm=t["tm_merge"])
            cnt = counts[0, :N_EXPERTS].astype(jnp.int32)
            ntile_e = (cnt + t_moe - 1) // t_moe
            ends = jnp.cumsum(ntile_e) * t_moe
            starts = ends - ntile_e * t_moe
            idx = route[:, 0:2].astype(jnp.int32)
            rank = route[:, 4:6].astype(jnp.int32)
            pos = starts[idx] + rank
            tile_first = jnp.arange(n_tiles, dtype=jnp.int32) * t_moe
            tile_expert = jnp.minimum(jnp.sum(tile_first[:, None] >= ends[None, :], axis=1),
                                      N_EXPERTS - 1).astype(jnp.int32)
            n_used = (ends[-1:] // t_moe).astype(jnp.int32)
            xs = _dispatch(pos.reshape(n // t["td"], 1, 2 * t["td"]), h2t, n_slots, td=t["td"])
            y = _ffn_grouped(tile_expert, n_used, xs, exp_w1b[i], exp_w3b[i], exp_w2b[i], t=t_moe, fc=t["fc_moe"])
            x2 = _combine(pos.reshape(n // t["tc"], 1, 2 * t["tc"]), xn, route, y, tc=t["tc"])
        x = x2.reshape(b, s, d)
    return x


def kernel(x, g_mix, w_in, b_gate, b_forget, pool_w, pool_scale, conv_w, conv_b, conv_ln_g, conv_ln_b,
           q_norm_g, k_norm_g, w_br_pool, w_br_conv, w_br_attn, w_out, g_ffn, ffn_w1, ffn_w3, ffn_w2,
           router_w, exp_w1, exp_w3, exp_w2):
    return _forward(x, g_mix, w_in, b_gate, b_forget, pool_w, pool_scale, conv_w, conv_b, conv_ln_g, conv_ln_b,
                    q_norm_g, k_norm_g, w_br_pool, w_br_conv, w_br_attn, w_out, g_ffn, ffn_w1, ffn_w3, ffn_w2,
                    router_w, exp_w1, exp_w3, exp_w2, _tiles(x.shape[0], x.shape[1]))
```

```python
import functools

import jax
import jax.numpy as jnp
import numpy as np
from jax import lax
from jax.experimental import pallas as pl
from jax.experimental.pallas import tpu as pltpu

F32 = jnp.float32
BF16 = jnp.bfloat16

EPS = 1e-6
LOG2E = 1.4426950408889634
POOL_WINDOWS = (2, 4, 8, 16)
POOL_WIDTH = 256
POOL_GROUP_DIM = 64
CONV_WIDTH = 256
CONV_KERNEL = 31
ATTN_HEADS = 8
HEAD_DIM = 64
ATTN_WIDTH = ATTN_HEADS * HEAD_DIM
N_BRANCH = 3
N_EXPERTS = 8
LANES = 128
SUBLANES = 8
HALO = 32
VMEM_LIMIT = 56 * 1024 * 1024


def _params(*sem):
    return pltpu.CompilerParams(dimension_semantics=sem, vmem_limit_bytes=VMEM_LIMIT)


def _const_spec(shape):
    nd = len(shape)
    return pl.BlockSpec(shape, lambda *_: (0,) * nd, pipeline_mode=pl.Buffered(1))


def _local_mixers(p, z, si, wpd_ref, pscale_ref, wlane_ref, cw_ref, cb_ref, lng_ref, lnb_ref, pbuf, zbuf):
    tm = p.shape[0]
    pbuf[HALO:HALO + tm, :] = p
    zbuf[HALO:HALO + tm, :] = z

    wlane = wlane_ref[...]
    acc = p
    pooled_sum = jnp.zeros_like(p)
    for j in range(1, max(POOL_WINDOWS)):
        acc = acc + pbuf[HALO - j:HALO - j + tm, :]
        if (j + 1) in POOL_WINDOWS:
            pooled_sum = jnp.where(wlane == float(j + 1), acc, pooled_sum)
    t1 = (lax.broadcasted_iota(jnp.int32, (tm, 1), 0) + si * tm + 1).astype(F32)
    count = jnp.minimum(t1, wlane)
    pooled = pooled_sum / count - p
    yp = jnp.dot(pooled.astype(BF16), wpd_ref[...], preferred_element_type=F32) * pscale_ref[...]

    conv = jnp.zeros((tm, CONV_WIDTH), F32) + cb_ref[...]
    for kk in range(CONV_KERNEL):
        off = HALO - (CONV_KERNEL - 1) + kk
        conv = conv + zbuf[off:off + tm, :] * cw_ref[kk:kk + 1, :]
    mu = jnp.mean(conv, axis=-1, keepdims=True)
    cen = conv - mu
    var = jnp.mean(cen * cen, axis=-1, keepdims=True)
    y = cen * lax.rsqrt(var + EPS) * lng_ref[...] + lnb_ref[...]

    pbuf[0:HALO, :] = pbuf[tm:tm + HALO, :]
    zbuf[0:HALO, :] = zbuf[tm:tm + HALO, :]
    return yp.astype(BF16), (y * jax.nn.sigmoid(y)).astype(BF16)


def _inproj_kernel(x_ref, g_ref, wloc_ref, wqk_ref, wv_ref, wf_ref, wg_ref, bg_ref, bf_ref, qg_ref, kg_ref, bd_ref,
                   pq_ref, pk_ref, oq_ref, ok_ref, ov_ref,
                   wpd_ref, pscale_ref, wlane_ref, cw_ref, cb_ref, lng_ref, lnb_ref,
                   yp_ref, yc_ref, q_ref, k_ref, v_ref, gates_ref, carry_ref, pbuf, zbuf):
    tm = x_ref.shape[1]
    si = pl.program_id(1)

    @pl.when(si == 0)
    def _():
        carry_ref[...] = jnp.zeros_like(carry_ref)
        pbuf[0:HALO, :] = jnp.zeros((HALO, POOL_WIDTH), F32)
        zbuf[0:HALO, :] = jnp.zeros((HALO, CONV_WIDTH), F32)

    x = x_ref[0]
    ms = jnp.mean(x * x, axis=-1, keepdims=True)
    h = (x * lax.rsqrt(ms + EPS) * g_ref[...]).astype(BF16)

    loc = jnp.dot(h, wloc_ref[...], preferred_element_type=F32)
    z = loc[:, POOL_WIDTH:POOL_WIDTH + CONV_WIDTH] * jax.nn.sigmoid(loc[:, POOL_WIDTH + CONV_WIDTH:])

    fl = jnp.dot(h, wf_ref[...], preferred_element_type=F32) + bf_ref[...]
    cs = jnp.minimum(fl, 0.0) - jnp.log1p(jnp.exp(-jnp.abs(fl)))
    row = lax.broadcasted_iota(jnp.int32, cs.shape, 0)
    shift = 1
    while shift < tm:
        cs = cs + jnp.where(row >= shift, pltpu.roll(cs, shift, axis=0), 0.0)
        shift *= 2
    cs = cs + carry_ref[0:1, :]
    carry_ref[0:1, :] = cs[tm - 1:tm, :]
    c2 = cs * LOG2E
    hi = c2.astype(BF16)
    rem = c2 - hi.astype(F32)
    mid = rem.astype(BF16)
    lo = (rem - mid.astype(F32)).astype(BF16)
    lane = lax.broadcasted_iota(jnp.int32, cs.shape, 1)
    cpack = jnp.where(lane < ATTN_HEADS, hi, jnp.where(lane < 2 * ATTN_HEADS, mid, lo))

    qk = jnp.dot(h, wqk_ref[...], preferred_element_type=F32)
    q = qk[:, :ATTN_WIDTH]
    k = qk[:, ATTN_WIDTH:]
    qss = jnp.dot((q * q).astype(BF16), bd_ref[...], preferred_element_type=F32)
    kss = jnp.dot((k * k).astype(BF16), bd_ref[...], preferred_element_type=F32)
    qn = (q * lax.rsqrt(qss + EPS) * qg_ref[...]).astype(BF16)
    kn = (k * lax.rsqrt(kss + EPS) * kg_ref[...]).astype(BF16)
    q_ref[0] = (jnp.dot(jnp.concatenate([qn, cpack], axis=1), pq_ref[...], preferred_element_type=F32)
                + oq_ref[...]).astype(BF16)
    k_ref[0] = (jnp.dot(jnp.concatenate([kn, cpack], axis=1), pk_ref[...], preferred_element_type=F32)
                + ok_ref[...]).astype(BF16)
    v_ref[0] = (jnp.dot(h, wv_ref[...], preferred_element_type=F32) + ov_ref[...]).T.astype(BF16)

    gl = jnp.dot(h, wg_ref[...], preferred_element_type=F32) + bg_ref[...]
    gates_ref[0] = jax.nn.sigmoid(gl).astype(BF16)

    yp_ref[0], yc_ref[0] = _local_mixers(loc[:, :POOL_WIDTH], z, si, wpd_ref, pscale_ref, wlane_ref, cw_ref,
                                         cb_ref, lng_ref, lnb_ref, pbuf, zbuf)


def _inproj(x, *consts, tm):
    b, s, d = x.shape
    grid = (b, s // tm)
    row = lambda w: pl.BlockSpec((1, tm, w), lambda bi, si: (bi, si, 0))
    head_w = ATTN_HEADS * LANES
    out_shape = (
        jax.ShapeDtypeStruct((b, s, POOL_WIDTH), BF16),
        jax.ShapeDtypeStruct((b, s, CONV_WIDTH), BF16),
        jax.ShapeDtypeStruct((b, s, head_w), BF16),
        jax.ShapeDtypeStruct((b, s, head_w), BF16),
        jax.ShapeDtypeStruct((b, head_w, s), BF16),
        jax.ShapeDtypeStruct((b, s, N_BRANCH * d), BF16),
    )
    vt_spec = pl.BlockSpec((1, head_w, tm), lambda bi, si: (bi, 0, si))
    return pl.pallas_call(
        _inproj_kernel,
        grid=grid,
        in_specs=[row(d)] + [_const_spec(a.shape) for a in consts],
        out_specs=(row(POOL_WIDTH), row(CONV_WIDTH), row(head_w), row(head_w), vt_spec, row(N_BRANCH * d)),
        out_shape=out_shape,
        scratch_shapes=[pltpu.VMEM((SUBLANES, LANES), F32), pltpu.VMEM((HALO + tm, POOL_WIDTH), F32),
                        pltpu.VMEM((HALO + tm, CONV_WIDTH), F32)],
        compiler_params=_params("arbitrary", "arbitrary"),
        name="inproj",
    )(x, *consts)


def _attn_kernel(q_ref, k_ref, v_ref, o_ref, m_ref, acc_ref, s_ref, tmax_ref):
    tq = q_ref.shape[1]
    nh, _, tk, _ = s_ref.shape
    qi = pl.program_id(2)
    m_ref[...] = jnp.full(m_ref.shape, -jnp.inf, F32)
    acc_ref[...] = jnp.zeros(acc_ref.shape, F32)

    def scores(j, slot, diag_offset=None):
        koff = pl.multiple_of(j * tk, tk)
        for h in range(nh):
            lanes = slice(h * LANES, (h + 1) * LANES)
            s = lax.dot_general(k_ref[0, pl.ds(koff, tk), lanes], q_ref[0, :, lanes], (((1,), (1,)), ((), ())),
                                preferred_element_type=F32)
            if diag_offset is not None:
                kpos = lax.broadcasted_iota(jnp.int32, (tk, tq), 0) + diag_offset
                s = jnp.where(kpos <= lax.broadcasted_iota(jnp.int32, (tk, tq), 1), s, -jnp.inf)
            s_ref[h, slot] = s
            tmax_ref[h, slot] = jnp.max(s, axis=0, keepdims=True)

    def consume(j, slot):
        koff = pl.multiple_of(j * tk, tk)
        for h in range(nh):
            m_old = m_ref[h]
            m_new = jnp.maximum(m_old, tmax_ref[h, slot])
            p = jnp.exp2((s_ref[h, slot] - m_new).astype(BF16))
            pv = jnp.dot(v_ref[0, h * LANES:(h + 1) * LANES, pl.ds(koff, tk)], p, preferred_element_type=F32)
            acc_ref[h] = jnp.exp2(m_old - m_new) * acc_ref[h] + pv
            m_ref[h] = m_new

    def pair(i, carry):
        scores(2 * i + 1, 1)
        consume(2 * i, 0)
        scores(2 * i + 2, 0)
        consume(2 * i + 1, 1)
        return carry

    @pl.when(qi == 0)
    def _():
        scores(0, 0, diag_offset=0)

    @pl.when(qi > 0)
    def _():
        scores(0, 0)
        lax.fori_loop(0, qi - 1, pair, 0)
        scores(2 * qi - 1, 1)
        consume(2 * qi - 2, 0)
        scores(2 * qi, 0, diag_offset=0)
        consume(2 * qi - 1, 1)

    scores(2 * qi + 1, 1, diag_offset=tk)
    consume(2 * qi, 0)
    consume(2 * qi + 1, 1)

    for h in range(nh):
        acc = acc_ref[h]
        out_t = acc[:HEAD_DIM, :] / acc[HEAD_DIM:HEAD_DIM + 1, :]
        o_ref[0, :, h * LANES:(h + 1) * LANES] = jnp.concatenate(
            [out_t, jnp.zeros_like(out_t)], axis=0).T.astype(BF16)


def _attention(q, k, v, *, tq, nh):
    b, s, _ = q.shape
    tk = tq // 2
    w = nh * LANES
    return pl.pallas_call(
        _attn_kernel,
        grid=(b, ATTN_HEADS // nh, s // tq),
        in_specs=[
            pl.BlockSpec((1, tq, w), lambda bi, hi, qi: (bi, qi, hi)),
            pl.BlockSpec((1, s, w), lambda bi, hi, qi: (bi, 0, hi)),
            pl.BlockSpec((1, w, s), lambda bi, hi, qi: (bi, hi, 0)),
        ],
        out_specs=pl.BlockSpec((1, tq, w), lambda bi, hi, qi: (bi, qi, hi)),
        out_shape=jax.ShapeDtypeStruct((b, s, ATTN_HEADS * LANES), BF16),
        scratch_shapes=[pltpu.VMEM((nh, 1, tq), F32), pltpu.VMEM((nh, LANES, tq), F32),
                        pltpu.VMEM((nh, 2, tk, tq), F32), pltpu.VMEM((nh, 2, 1, tq), F32)],
        compiler_params=_params("arbitrary", "arbitrary", "arbitrary"),
        name="fox_attention",
    )(q, k, v)


def _merge_core(x_ref, yp_ref, yc_ref, ya_ref, gates_ref, wp_ref, wc_ref, wa_ref, wo_ref, gf_ref):
    d = x_ref.shape[1]
    gates = gates_ref[...]
    merged = (gates[:, :d].astype(F32) * jnp.dot(yp_ref[...], wp_ref[...], preferred_element_type=F32)
              + gates[:, d:2 * d].astype(F32) * jnp.dot(yc_ref[...], wc_ref[...], preferred_element_type=F32)
              + gates[:, 2 * d:].astype(F32) * jnp.dot(ya_ref[...], wa_ref[...], preferred_element_type=F32))
    xn = x_ref[...] + jnp.dot(merged.astype(BF16), wo_ref[...], preferred_element_type=F32)
    ms = jnp.mean(xn * xn, axis=-1, keepdims=True)
    h2 = xn * lax.rsqrt(ms + EPS) * gf_ref[...]
    return xn, h2


def _merge_dense_kernel(x_ref, yp_ref, yc_ref, ya_ref, gates_ref, wp_ref, wc_ref, wa_ref, wo_ref, gf_ref,
                        xn_ref, h2_ref):
    xn, h2 = _merge_core(x_ref, yp_ref, yc_ref, ya_ref, gates_ref, wp_ref, wc_ref, wa_ref, wo_ref, gf_ref)
    xn_ref[...] = xn
    h2_ref[...] = h2.astype(BF16)


def _merge_moe_kernel(x_ref, yp_ref, yc_ref, ya_ref, gates_ref, wp_ref, wc_ref, wa_ref, wo_ref, gf_ref,
                      wr_ref, tri_ref, xn_ref, h2t_ref, route_ref, counts_ref, carry_ref):
    tm, d = x_ref.shape

    @pl.when(pl.program_id(0) == 0)
    def _():
        carry_ref[...] = jnp.zeros_like(carry_ref)

    xn, h2 = _merge_core(x_ref, yp_ref, yc_ref, ya_ref, gates_ref, wp_ref, wc_ref, wa_ref, wo_ref, gf_ref)
    xn_ref[...] = xn
    for c in range(d // LANES):
        h2t_ref[pl.ds(c, tm, stride=SUBLANES), :] = h2[:, c * LANES:(c + 1) * LANES]

    h_hi = h2.astype(BF16)
    h_lo = (h2 - h_hi.astype(F32)).astype(BF16)
    both = jnp.dot(h_hi, wr_ref[...], preferred_element_type=F32)
    logits = (both[:, :LANES] + both[:, LANES:]
              + jnp.dot(h_lo, wr_ref[:, :LANES], preferred_element_type=F32))
    lane = lax.broadcasted_iota(jnp.int32, (tm, LANES), 1)
    lanef = lane.astype(F32)
    lg = jnp.where(lane < N_EXPERTS, logits, -jnp.inf)
    m1 = jnp.max(lg, axis=-1, keepdims=True)
    i1 = jnp.min(jnp.where(lg == m1, lanef, float(LANES)), axis=-1, keepdims=True)
    lg2 = jnp.where(lanef == i1, -jnp.inf, lg)
    m2 = jnp.max(lg2, axis=-1, keepdims=True)
    i2 = jnp.min(jnp.where(lg2 == m2, lanef, float(LANES)), axis=-1, keepdims=True)
    e2 = jnp.exp(m2 - m1)
    w1 = 1.0 / (1.0 + e2)
    w2 = e2 / (1.0 + e2)

    is1 = lanef == i1
    is2 = lanef == i2
    sel = jnp.where(is1 | is2, 1.0, 0.0)
    rank = jnp.dot(tri_ref[...], sel.astype(BF16), preferred_element_type=F32) + carry_ref[0:1, :]
    r1 = jnp.sum(jnp.where(is1, rank, 0.0), axis=-1, keepdims=True)
    r2 = jnp.sum(jnp.where(is2, rank, 0.0), axis=-1, keepdims=True)
    total = rank[tm - 1:tm, :] + sel[tm - 1:tm, :]
    carry_ref[0:1, :] = total
    counts_ref[...] = jnp.broadcast_to(total, counts_ref.shape)

    route = jnp.zeros((tm, LANES), F32)
    for li, val in enumerate((i1, i2, w1, w2, r1, r2)):
        route = jnp.where(lane == li, val, route)
    route_ref[...] = route


def _merge(x2, yp, yc, ya, gates, wp, wc, wa, wo, gf, wr=None, tri=None, *, tm):
    n, d = x2.shape
    row = lambda w: pl.BlockSpec((tm, w), lambda i: (i, 0))
    ins = [x2, yp, yc, ya, gates, wp, wc, wa, wo, gf]
    in_specs = [row(d), row(POOL_WIDTH), row(CONV_WIDTH), row(ya.shape[1]), row(N_BRANCH * d)] + [
        _const_spec(a.shape) for a in (wp, wc, wa, wo, gf)]
    if wr is None:
        return pl.pallas_call(
            _merge_dense_kernel,
            grid=(n // tm,),
            in_specs=in_specs,
            out_specs=(row(d), row(d)),
            out_shape=(jax.ShapeDtypeStruct((n, d), F32), jax.ShapeDtypeStruct((n, d), BF16)),
            compiler_params=_params("arbitrary"),
            name="merge_dense",
        )(*ins)
    return pl.pallas_call(
        _merge_moe_kernel,
        grid=(n // tm,),
        in_specs=in_specs + [_const_spec(wr.shape), _const_spec(tri.shape)],
        out_specs=(row(d), pl.BlockSpec((tm * SUBLANES, LANES), lambda i: (i, 0)), row(LANES),
                   pl.BlockSpec((SUBLANES, LANES), lambda i: (0, 0))),
        out_shape=(jax.ShapeDtypeStruct((n, d), F32), jax.ShapeDtypeStruct((n * SUBLANES, LANES), F32),
                   jax.ShapeDtypeStruct((n, LANES), F32), jax.ShapeDtypeStruct((SUBLANES, LANES), F32)),
        scratch_shapes=[pltpu.VMEM((SUBLANES, LANES), F32)],
        compiler_params=_params("arbitrary"),
        name="merge_moe",
    )(*ins, wr, tri)


FF_SUB = 256


def _swiglu_accumulate(xb, w1_ref, w3_ref, w2_ref, acc_ref):
    fc = w1_ref.shape[2]
    for start in range(0, fc, FF_SUB):
        cols = slice(start, min(start + FF_SUB, fc))
        a = jnp.dot(xb, w1_ref[0, :, cols], preferred_element_type=F32)
        b = jnp.dot(xb, w3_ref[0, :, cols], preferred_element_type=F32)
        hmid = (a * jax.nn.sigmoid(a) * b).astype(BF16)
        acc_ref[...] += jnp.dot(hmid, w2_ref[0, cols, :], preferred_element_type=F32)


def _ffn_dense_kernel(x_ref, h_ref, w1_ref, w3_ref, w2_ref, o_ref, acc_ref):
    f = pl.program_id(1)

    @pl.when(f == 0)
    def _():
        acc_ref[...] = x_ref[...]

    _swiglu_accumulate(h_ref[...], w1_ref, w3_ref, w2_ref, acc_ref)

    @pl.when(f == pl.num_programs(1) - 1)
    def _():
        o_ref[...] = acc_ref[...]


def _ffn_dense(x2, h2, w1, w3, w2, *, tm, fc):
    n, d = x2.shape
    ff = w1.shape[2]
    row = lambda: pl.BlockSpec((tm, d), lambda i, f: (i, 0))
    once = dict(pipeline_mode=pl.Buffered(1)) if fc == ff else {}
    return pl.pallas_call(
        _ffn_dense_kernel,
        grid=(n // tm, ff // fc),
        in_specs=[row(), row(),
                  pl.BlockSpec((1, d, fc), lambda i, f: (0, 0, f), **once),
                  pl.BlockSpec((1, d, fc), lambda i, f: (0, 0, f), **once),
                  pl.BlockSpec((1, fc, d), lambda i, f: (0, f, 0), **once)],
        out_specs=row(),
        out_shape=jax.ShapeDtypeStruct((n, d), F32),
        scratch_shapes=[pltpu.VMEM((tm, d), F32)],
        compiler_params=_params("arbitrary", "arbitrary"),
        name="ffn_dense",
    )(x2, h2, w1, w3, w2)


def _ffn_grouped_kernel(te_ref, nused_ref, xs_ref, w1_ref, w3_ref, w2_ref, y_ref, xb_ref, acc_ref):
    t, d = xb_ref.shape
    i = pl.program_id(0)
    f = pl.program_id(1)
    live = i < nused_ref[0]

    @pl.when(f == 0)
    def _():
        for c in range(d // LANES):
            xb_ref[:, c * LANES:(c + 1) * LANES] = xs_ref[pl.ds(c, t, stride=SUBLANES), :].astype(BF16)
        acc_ref[...] = jnp.zeros_like(acc_ref)

    @pl.when(live)
    def _():
        _swiglu_accumulate(xb_ref[...], w1_ref, w3_ref, w2_ref, acc_ref)

    @pl.when(f == pl.num_programs(1) - 1)
    def _():
        for c in range(d // LANES):
            y_ref[pl.ds(c, t, stride=SUBLANES), :] = acc_ref[:, c * LANES:(c + 1) * LANES]


def _ffn_grouped(tile_expert, n_used, xs, w1, w3, w2, *, t, fc):
    rows, _ = xs.shape
    _, d, ff = w1.shape
    n_tiles = rows // (t * SUBLANES)
    blk = pl.BlockSpec((t * SUBLANES, LANES), lambda i, f, te, nu: (i, 0))
    grid_spec = pltpu.PrefetchScalarGridSpec(
        num_scalar_prefetch=2,
        grid=(n_tiles, ff // fc),
        in_specs=[blk,
                  pl.BlockSpec((1, d, fc), lambda i, f, te, nu: (te[i], 0, f)),
                  pl.BlockSpec((1, d, fc), lambda i, f, te, nu: (te[i], 0, f)),
                  pl.BlockSpec((1, fc, d), lambda i, f, te, nu: (te[i], f, 0))],
        out_specs=blk,
        scratch_shapes=[pltpu.VMEM((t, d), BF16), pltpu.VMEM((t, d), F32)],
    )
    return pl.pallas_call(
        _ffn_grouped_kernel,
        grid_spec=grid_spec,
        out_shape=jax.ShapeDtypeStruct(xs.shape, F32),
        compiler_params=_params("arbitrary", "arbitrary"),
        name="ffn_grouped",
    )(tile_expert, n_used, xs, w1, w3, w2)


def _dispatch_kernel(pos_ref, h_ref, xs_in, xs_hbm, sem):
    del xs_in
    td = pos_ref.shape[2] // 2

    def row_copy(tt, kk):
        slot = pos_ref[0, 0, 2 * tt + kk]
        return pltpu.make_async_copy(h_ref.at[pl.ds(tt * SUBLANES, SUBLANES)],
                                     xs_hbm.at[pl.ds(slot * SUBLANES, SUBLANES)], sem)

    def issue(tt, carry):
        for kk in range(2):
            row_copy(tt, kk).start()
        return carry

    lax.fori_loop(0, td, issue, 0, unroll=4)
    for _ in range(2):
        pltpu.make_async_copy(h_ref, xs_hbm.at[pl.ds(0, td * SUBLANES)], sem).wait()


def _dispatch(pos2, h2t, n_slots, *, td):
    n = h2t.shape[0] // SUBLANES
    xs0 = jnp.zeros((n_slots * SUBLANES, LANES), F32)
    return pl.pallas_call(
        _dispatch_kernel,
        grid=(n // td,),
        in_specs=[pl.BlockSpec((1, 1, 2 * td), lambda i: (i, 0, 0), memory_space=pltpu.SMEM),
                  pl.BlockSpec((td * SUBLANES, LANES), lambda i: (i, 0)),
                  pl.BlockSpec(memory_space=pl.ANY)],
        out_specs=pl.BlockSpec(memory_space=pl.ANY),
        out_shape=jax.ShapeDtypeStruct(xs0.shape, F32),
        scratch_shapes=[pltpu.SemaphoreType.DMA(())],
        input_output_aliases={2: 0},
        compiler_params=_params("arbitrary"),
        name="moe_dispatch",
    )(pos2, h2t, xs0)


def _combine_kernel(pos_ref, posn_ref, x_ref, route_ref, y_hbm, o_ref, buf, sem):
    tc, d = x_ref.shape
    i = pl.program_id(0)
    cur = i % 2

    def row_copy(p_ref, buf_slot, tt, kk):
        slot = p_ref[0, 0, 2 * tt + kk]
        return pltpu.make_async_copy(y_hbm.at[pl.ds(slot * SUBLANES, SUBLANES)],
                                     buf.at[buf_slot, kk, pl.ds(tt * SUBLANES, SUBLANES)], sem.at[buf_slot])

    def issue_all(p_ref, buf_slot):
        def issue(tt, carry):
            for kk in range(2):
                row_copy(p_ref, buf_slot, tt, kk).start()
            return carry

        lax.fori_loop(0, tc, issue, 0, unroll=4)

    @pl.when(i == 0)
    def _():
        issue_all(pos_ref, 0)

    @pl.when(i + 1 < pl.num_programs(0))
    def _():
        issue_all(posn_ref, 1 - cur)

    for kk in range(2):
        pltpu.make_async_copy(y_hbm.at[pl.ds(0, tc * SUBLANES)], buf.at[cur, kk], sem.at[cur]).wait()

    route = route_ref[...]
    w1 = route[:, 2:3]
    w2 = route[:, 3:4]
    for c in range(d // LANES):
        y1 = buf[cur, 0, pl.ds(c, tc, stride=SUBLANES), :]
        y2 = buf[cur, 1, pl.ds(c, tc, stride=SUBLANES), :]
        o_ref[:, c * LANES:(c + 1) * LANES] = x_ref[:, c * LANES:(c + 1) * LANES] + (w1 * y1 + w2 * y2)


def _combine(pos2, x2, route, y, *, tc):
    n, d = x2.shape
    last = n // tc - 1
    return pl.pallas_call(
        _combine_kernel,
        grid=(n // tc,),
        in_specs=[pl.BlockSpec((1, 1, 2 * tc), lambda i: (i, 0, 0), memory_space=pltpu.SMEM),
                  pl.BlockSpec((1, 1, 2 * tc), lambda i: (jnp.minimum(i + 1, last), 0, 0), memory_space=pltpu.SMEM),
                  pl.BlockSpec((tc, d), lambda i: (i, 0)),
                  pl.BlockSpec((tc, LANES), lambda i: (i, 0)),
                  pl.BlockSpec(memory_space=pl.ANY)],
        out_specs=pl.BlockSpec((tc, d), lambda i: (i, 0)),
        out_shape=jax.ShapeDtypeStruct((n, d), F32),
        scratch_shapes=[pltpu.VMEM((2, 2, tc * SUBLANES, LANES), F32), pltpu.SemaphoreType.DMA((2,))],
        compiler_params=_params("arbitrary"),
        name="moe_combine",
    )(pos2, pos2, x2, route, y)


def _tiles(b, s):
    n = b * s
    pick = lambda full, unit: full if unit % full == 0 else unit
    return dict(
        tm_in=pick(512, s), tq=pick(1024, s), tm_merge=pick(512, n),
        tm_ffn=pick(1024, n), fc_dense=2816, t_moe=pick(1024, n) if n >= 8192 else 256, fc_moe=1792,
        td=pick(512, n), tc=pick(256, n),
    )


def _head_layout_constants():
    pq = np.zeros((ATTN_WIDTH + LANES, ATTN_HEADS * LANES), np.float32)
    pk = np.zeros_like(pq)
    oq = np.zeros((1, ATTN_HEADS * LANES), np.float32)
    ok = np.zeros_like(oq)
    ov = np.zeros_like(oq)
    for h in range(ATTN_HEADS):
        base = h * LANES
        for j in range(HEAD_DIM):
            pq[h * HEAD_DIM + j, base + j] = 1.0
            pk[h * HEAD_DIM + j, base + j] = 1.0
        for part in range(3):
            pq[ATTN_WIDTH + part * ATTN_HEADS + h, base + HEAD_DIM + part] = 1.0
            pk[ATTN_WIDTH + part * ATTN_HEADS + h, base + HEAD_DIM + 3 + part] = -1.0
            oq[0, base + HEAD_DIM + 3 + part] = 1.0
            ok[0, base + HEAD_DIM + part] = 1.0
        ov[0, base + HEAD_DIM:base + LANES] = 1.0
    return (jnp.asarray(pq, BF16), jnp.asarray(pk, BF16), jnp.asarray(oq), jnp.asarray(ok), jnp.asarray(ov))


def _block_diag(blocks):
    g, r, c = blocks.shape
    eye = jnp.eye(g, dtype=blocks.dtype)
    return (eye[:, None, :, None] * blocks[:, :, None, :]).reshape(g * r, g * c)


def _forward(x, g_mix, w_in, b_gate, b_forget, pool_w, pool_scale, conv_w, conv_b, conv_ln_g, conv_ln_b,
             q_norm_g, k_norm_g, w_br_pool, w_br_conv, w_br_attn, w_out, g_ffn, ffn_w1, ffn_w3, ffn_w2,
             router_w, exp_w1, exp_w3, exp_w2, tiles):
    b, s, d = x.shape
    n = b * s
    depth = w_in.shape[0]
    t = tiles

    off_conv = POOL_WIDTH
    off_q = off_conv + 2 * CONV_WIDTH
    off_f = off_q + 3 * ATTN_WIDTH
    off_g = off_f + ATTN_HEADS

    wlane = jnp.repeat(jnp.asarray(POOL_WINDOWS, F32), POOL_GROUP_DIM)[None, :]
    bd = _block_diag(jnp.full((ATTN_HEADS, HEAD_DIM, HEAD_DIM), 1.0 / HEAD_DIM, F32)).astype(BF16)
    tri = (lax.broadcasted_iota(jnp.int32, (t["tm_merge"], t["tm_merge"]), 1)
           < lax.broadcasted_iota(jnp.int32, (t["tm_merge"], t["tm_merge"]), 0)).astype(BF16)
    ffn_w1b, ffn_w3b, ffn_w2b = ffn_w1.astype(BF16), ffn_w3.astype(BF16), ffn_w2.astype(BF16)
    exp_w1b, exp_w3b, exp_w2b = exp_w1.astype(BF16), exp_w3.astype(BF16), exp_w2.astype(BF16)

    t_moe = t["t_moe"]
    n_tiles = (2 * n) // t_moe + N_EXPERTS
    n_slots = n_tiles * t_moe

    pq, pk, oq, ok, ov = _head_layout_constants()

    def head_pad(w, axis):
        shp = w.shape
        w = w.reshape(shp[:axis] + (ATTN_HEADS, HEAD_DIM) + shp[axis + 1:])
        pad = [(0, 0)] * w.ndim
        pad[axis + 1] = (0, LANES - HEAD_DIM)
        return jnp.pad(w, pad).reshape(shp[:axis] + (ATTN_HEADS * LANES,) + shp[axis + 1:])

    for l in range(depth):
        wl = w_in[l]
        wloc = wl[:, :off_q].astype(BF16)
        wqk = wl[:, off_q:off_q + 2 * ATTN_WIDTH].astype(BF16)
        wv = head_pad(wl[:, off_q + 2 * ATTN_WIDTH:off_f], 1).astype(BF16)
        wf3 = jnp.pad(jnp.tile(wl[:, off_f:off_g], (1, 3)), ((0, 0), (0, LANES - 3 * ATTN_HEADS))).astype(BF16)
        bf3 = jnp.pad(jnp.tile(b_forget[l], 3), (0, LANES - 3 * ATTN_HEADS))[None, :]
        wg = wl[:, off_g:].astype(BF16)
        qg = jnp.tile(q_norm_g[l], ATTN_HEADS)[None, :] * (HEAD_DIM ** -0.5 * LOG2E)
        kg = jnp.tile(k_norm_g[l], ATTN_HEADS)[None, :]
        yp, yc, q, k, v, gates = _inproj(
            x, g_mix[l][None, :], wloc, wqk, wv, wf3, wg, b_gate[l][None, :], bf3, qg, kg, bd, pq, pk, oq, ok, ov,
            _block_diag(pool_w[l]).astype(BF16), pool_scale[l][None, :], wlane,
            conv_w[l], conv_b[l][None, :], conv_ln_g[l][None, :], conv_ln_b[l][None, :],
            tm=t["tm_in"])

        ya = _attention(q, k, v, tq=t["tq"], nh=2)

        merge_args = (x.reshape(n, d), yp.reshape(n, -1), yc.reshape(n, -1), ya.reshape(n, -1),
                      gates.reshape(n, -1), w_br_pool[l].astype(BF16), w_br_conv[l].astype(BF16),
                      head_pad(w_br_attn[l], 0).astype(BF16), w_out[l].astype(BF16), g_ffn[l][None, :])
        i = l // 2
        if l % 2 == 0:
            xn, h2 = _merge(*merge_args, tm=t["tm_merge"])
            x2 = _ffn_dense(xn, h2, ffn_w1b[i:i + 1], ffn_w3b[i:i + 1], ffn_w2b[i:i + 1],
                            tm=t["tm_ffn"], fc=t["fc_dense"])
        else:
            wr32 = jnp.pad(router_w[i], ((0, 0), (0, LANES - N_EXPERTS)))
            wr_hi = wr32.astype(BF16)
            wr = jnp.concatenate([wr_hi, (wr32 - wr_hi.astype(F32)).astype(BF16)], axis=1)
            xn, h2t, route, counts = _merge(*merge_args, wr, tri, tm=t["tm_merge"])
            cnt = counts[0, :N_EXPERTS].astype(jnp.int32)
            ntile_e = (cnt + t_moe - 1) // t_moe
            ends = jnp.cumsum(ntile_e) * t_moe
            starts = ends - ntile_e * t_moe
            idx = route[:, 0:2].astype(jnp.int32)
            rank = route[:, 4:6].astype(jnp.int32)
            pos = starts[idx] + rank
            tile_first = jnp.arange(n_tiles, dtype=jnp.int32) * t_moe
            tile_expert = jnp.minimum(jnp.sum(tile_first[:, None] >= ends[None, :], axis=1),
                                      N_EXPERTS - 1).astype(jnp.int32)
            n_used = (ends[-1:] // t_moe).astype(jnp.int32)
            xs = _dispatch(pos.reshape(n // t["td"], 1, 2 * t["td"]), h2t, n_slots, td=t["td"])
            y = _ffn_grouped(tile_expert, n_used, xs, exp_w1b[i], exp_w3b[i], exp_w2b[i], t=t_moe, fc=t["fc_moe"])
            x2 = _combine(pos.reshape(n // t["tc"], 1, 2 * t["tc"]), xn, route, y, tc=t["tc"])
        x = x2.reshape(b, s, d)
    return x


def kernel(x, g_mix, w_in, b_gate, b_forget, pool_w, pool_scale, conv_w, conv_b, conv_ln_g, conv_ln_b,
           q_norm_g, k_norm_g, w_br_pool, w_br_conv, w_br_attn, w_out, g_ffn, ffn_w1, ffn_w3, ffn_w2,
           router_w, exp_w1, exp_w3, exp_w2):
    return _forward(x, g_mix, w_in, b_gate, b_forget, pool_w, pool_scale, conv_w, conv_b, conv_ln_g, conv_ln_b,
                    q_norm_g, k_norm_g, w_br_pool, w_br_conv, w_br_attn, w_out, g_ffn, ffn_w1, ffn_w3, ffn_w2,
                    router_w, exp_w1, exp_w3, exp_w2, _tiles(x.shape[0], x.shape[1]))
```

```python
import functools

import jax
import jax.numpy as jnp
import numpy as np
from jax import lax
from jax.experimental import pallas as pl
from jax.experimental.pallas import tpu as pltpu

F32 = jnp.float32
BF16 = jnp.bfloat16

EPS = 1e-6
LOG2E = 1.4426950408889634
POOL_WINDOWS = (2, 4, 8, 16)
POOL_WIDTH = 256
POOL_GROUP_DIM = 64
CONV_WIDTH = 256
CONV_KERNEL = 31
ATTN_HEADS = 8
HEAD_DIM = 64
ATTN_WIDTH = ATTN_HEADS * HEAD_DIM
N_BRANCH = 3
N_EXPERTS = 8
LANES = 128
SUBLANES = 8
HALO = 32
VMEM_LIMIT = 56 * 1024 * 1024


def _params(*sem):
    return pltpu.CompilerParams(dimension_semantics=sem, vmem_limit_bytes=VMEM_LIMIT)


def _const_spec(shape):
    nd = len(shape)
    return pl.BlockSpec(shape, lambda *_: (0,) * nd, pipeline_mode=pl.Buffered(1))


def _local_mixers(p, z, si, wpd_ref, pscale_ref, wlane_ref, cw_ref, cb_ref, lng_ref, lnb_ref, pbuf, zbuf):
    tm = p.shape[0]
    pbuf[HALO:HALO + tm, :] = p
    zbuf[HALO:HALO + tm, :] = z

    wlane = wlane_ref[...]
    acc = p
    pooled_sum = jnp.zeros_like(p)
    for j in range(1, max(POOL_WINDOWS)):
        acc = acc + pbuf[HALO - j:HALO - j + tm, :]
        if (j + 1) in POOL_WINDOWS:
            pooled_sum = jnp.where(wlane == float(j + 1), acc, pooled_sum)
    t1 = (lax.broadcasted_iota(jnp.int32, (tm, 1), 0) + si * tm + 1).astype(F32)
    count = jnp.minimum(t1, wlane)
    pooled = pooled_sum / count - p
    yp = jnp.dot(pooled.astype(BF16), wpd_ref[...], preferred_element_type=F32) * pscale_ref[...]

    conv = jnp.zeros((tm, CONV_WIDTH), F32) + cb_ref[...]
    for kk in range(CONV_KERNEL):
        off = HALO - (CONV_KERNEL - 1) + kk
        conv = conv + zbuf[off:off + tm, :] * cw_ref[kk:kk + 1, :]
    mu = jnp.mean(conv, axis=-1, keepdims=True)
    cen = conv - mu
    var = jnp.mean(cen * cen, axis=-1, keepdims=True)
    y = cen * lax.rsqrt(var + EPS) * lng_ref[...] + lnb_ref[...]

    pbuf[0:HALO, :] = pbuf[tm:tm + HALO, :]
    zbuf[0:HALO, :] = zbuf[tm:tm + HALO, :]
    return yp.astype(BF16), (y * jax.nn.sigmoid(y)).astype(BF16)


def _inproj_kernel(x_ref, g_ref, wloc_ref, wqk_ref, wv_ref, wf_ref, wg_ref, bg_ref, bf_ref, qg_ref, kg_ref, bd_ref,
                   pq_ref, pk_ref, oq_ref, ok_ref, ov_ref,
                   wpd_ref, pscale_ref, wlane_ref, cw_ref, cb_ref, lng_ref, lnb_ref,
                   yp_ref, yc_ref, q_ref, k_ref, v_ref, gates_ref, carry_ref, pbuf, zbuf):
    tm = x_ref.shape[1]
    si = pl.program_id(1)

    @pl.when(si == 0)
    def _():
        carry_ref[...] = jnp.zeros_like(carry_ref)
        pbuf[0:HALO, :] = jnp.zeros((HALO, POOL_WIDTH), F32)
        zbuf[0:HALO, :] = jnp.zeros((HALO, CONV_WIDTH), F32)

    x = x_ref[0]
    ms = jnp.mean(x * x, axis=-1, keepdims=True)
    h = (x * lax.rsqrt(ms + EPS) * g_ref[...]).astype(BF16)

    loc = jnp.dot(h, wloc_ref[...], preferred_element_type=F32)
    z = loc[:, POOL_WIDTH:POOL_WIDTH + CONV_WIDTH] * jax.nn.sigmoid(loc[:, POOL_WIDTH + CONV_WIDTH:])

    fl = jnp.dot(h, wf_ref[...], preferred_element_type=F32) + bf_ref[...]
    cs = jnp.minimum(fl, 0.0) - jnp.log1p(jnp.exp(-jnp.abs(fl)))
    row = lax.broadcasted_iota(jnp.int32, cs.shape, 0)
    shift = 1
    while shift < tm:
        cs = cs + jnp.where(row >= shift, pltpu.roll(cs, shift, axis=0), 0.0)
        shift *= 2
    cs = cs + carry_ref[0:1, :]
    carry_ref[0:1, :] = cs[tm - 1:tm, :]
    c2 = cs * LOG2E
    hi = c2.astype(BF16)
    rem = c2 - hi.astype(F32)
    mid = rem.astype(BF16)
    lo = (rem - mid.astype(F32)).astype(BF16)
    lane = lax.broadcasted_iota(jnp.int32, cs.shape, 1)
    cpack = jnp.where(lane < ATTN_HEADS, hi, jnp.where(lane < 2 * ATTN_HEADS, mid, lo))

    qk = jnp.dot(h, wqk_ref[...], preferred_element_type=F32)
    q = qk[:, :ATTN_WIDTH]
    k = qk[:, ATTN_WIDTH:]
    qss = jnp.dot((q * q).astype(BF16), bd_ref[...], preferred_element_type=F32)
    kss = jnp.dot((k * k).astype(BF16), bd_ref[...], preferred_element_type=F32)
    qn = (q * lax.rsqrt(qss + EPS) * qg_ref[...]).astype(BF16)
    kn = (k * lax.rsqrt(kss + EPS) * kg_ref[...]).astype(BF16)
    q_ref[0] = (jnp.dot(jnp.concatenate([qn, cpack], axis=1), pq_ref[...], preferred_element_type=F32)
                + oq_ref[...]).astype(BF16)
    k_ref[0] = (jnp.dot(jnp.concatenate([kn, cpack], axis=1), pk_ref[...], preferred_element_type=F32)
                + ok_ref[...]).astype(BF16)
    v_ref[0] = (jnp.dot(h, wv_ref[...], preferred_element_type=F32) + ov_ref[...]).T.astype(BF16)

    gl = jnp.dot(h, wg_ref[...], preferred_element_type=F32) + bg_ref[...]
    gates_ref[0] = jax.nn.sigmoid(gl).astype(BF16)

    yp_ref[0], yc_ref[0] = _local_mixers(loc[:, :POOL_WIDTH], z, si, wpd_ref, pscale_ref, wlane_ref, cw_ref,
                                         cb_ref, lng_ref, lnb_ref, pbuf, zbuf)


def _inproj(x, *consts, tm):
    b, s, d = x.shape
    grid = (b, s // tm)
    row = lambda w: pl.BlockSpec((1, tm, w), lambda bi, si: (bi, si, 0))
    head_w = ATTN_HEADS * LANES
    out_shape = (
        jax.ShapeDtypeStruct((b, s, POOL_WIDTH), BF16),
        jax.ShapeDtypeStruct((b, s, CONV_WIDTH), BF16),
        jax.ShapeDtypeStruct((b, s, head_w), BF16),
        jax.ShapeDtypeStruct((b, s, head_w), BF16),
        jax.ShapeDtypeStruct((b, head_w, s), BF16),
        jax.ShapeDtypeStruct((b, s, N_BRANCH * d), BF16),
    )
    vt_spec = pl.BlockSpec((1, head_w, tm), lambda bi, si: (bi, 0, si))
    return pl.pallas_call(
        _inproj_kernel,
        grid=grid,
        in_specs=[row(d)] + [_const_spec(a.shape) for a in consts],
        out_specs=(row(POOL_WIDTH), row(CONV_WIDTH), row(head_w), row(head_w), vt_spec, row(N_BRANCH * d)),
        out_shape=out_shape,
        scratch_shapes=[pltpu.VMEM((SUBLANES, LANES), F32), pltpu.VMEM((HALO + tm, POOL_WIDTH), F32),
                        pltpu.VMEM((HALO + tm, CONV_WIDTH), F32)],
        compiler_params=_params("arbitrary", "arbitrary"),
        name="inproj",
    )(x, *consts)


def _attn_kernel(q_ref, k_ref, v_ref, o_ref, m_ref, acc_ref, s_ref, tmax_ref):
    tq = q_ref.shape[1]
    nh, _, tk, _ = s_ref.shape
    qi = pl.program_id(2)
    m_ref[...] = jnp.full(m_ref.shape, -jnp.inf, F32)
    acc_ref[...] = jnp.zeros(acc_ref.shape, F32)

    def scores(j, slot, diag_offset=None, q0=0):
        koff = pl.multiple_of(j * tk, tk)
        for h in range(nh):
            lanes = slice(h * LANES, (h + 1) * LANES)
            s = lax.dot_general(k_ref[0, pl.ds(koff, tk), lanes], q_ref[0, q0:, lanes], (((1,), (1,)), ((), ())),
                                preferred_element_type=F32)
            if diag_offset is not None:
                kpos = lax.broadcasted_iota(jnp.int32, s.shape, 0) + diag_offset
                s = jnp.where(kpos <= lax.broadcasted_iota(jnp.int32, s.shape, 1) + q0, s, -jnp.inf)
            s_ref[h, slot, :, q0:] = s
            tmax_ref[h, slot, :, q0:] = jnp.max(s, axis=0, keepdims=True)

    def consume(j, slot, q0=0):
        koff = pl.multiple_of(j * tk, tk)
        for h in range(nh):
            m_old = m_ref[h, :, q0:]
            m_new = jnp.maximum(m_old, tmax_ref[h, slot, :, q0:])
            p = jnp.exp2((s_ref[h, slot, :, q0:] - m_new).astype(BF16))
            pv = jnp.dot(v_ref[0, h * LANES:(h + 1) * LANES, pl.ds(koff, tk)], p, preferred_element_type=F32)
            acc_ref[h, :, q0:] = jnp.exp2(m_old - m_new) * acc_ref[h, :, q0:] + pv
            m_ref[h, :, q0:] = m_new

    def pair(i, carry):
        scores(2 * i + 1, 1)
        consume(2 * i, 0)
        scores(2 * i + 2, 0)
        consume(2 * i + 1, 1)
        return carry

    @pl.when(qi == 0)
    def _():
        scores(0, 0, diag_offset=0)

    @pl.when(qi > 0)
    def _():
        scores(0, 0)
        lax.fori_loop(0, qi - 1, pair, 0)
        scores(2 * qi - 1, 1)
        consume(2 * qi - 2, 0)
        scores(2 * qi, 0, diag_offset=0)
        consume(2 * qi - 1, 1)

    scores(2 * qi + 1, 1, diag_offset=tk, q0=tk)
    consume(2 * qi, 0)
    consume(2 * qi + 1, 1, q0=tk)

    for h in range(nh):
        acc = acc_ref[h]
        out_t = acc[:HEAD_DIM, :] / acc[HEAD_DIM:HEAD_DIM + 1, :]
        o_ref[0, :, h * LANES:(h + 1) * LANES] = jnp.concatenate(
            [out_t, jnp.zeros_like(out_t)], axis=0).T.astype(BF16)


def _attention(q, k, v, *, tq, nh):
    b, s, _ = q.shape
    tk = tq // 2
    w = nh * LANES
    return pl.pallas_call(
        _attn_kernel,
        grid=(b, ATTN_HEADS // nh, s // tq),
        in_specs=[
            pl.BlockSpec((1, tq, w), lambda bi, hi, qi: (bi, qi, hi)),
            pl.BlockSpec((1, s, w), lambda bi, hi, qi: (bi, 0, hi)),
            pl.BlockSpec((1, w, s), lambda bi, hi, qi: (bi, hi, 0)),
        ],
        out_specs=pl.BlockSpec((1, tq, w), lambda bi, hi, qi: (bi, qi, hi)),
        out_shape=jax.ShapeDtypeStruct((b, s, ATTN_HEADS * LANES), BF16),
        scratch_shapes=[pltpu.VMEM((nh, 1, tq), F32), pltpu.VMEM((nh, LANES, tq), F32),
                        pltpu.VMEM((nh, 2, tk, tq), F32), pltpu.VMEM((nh, 2, 1, tq), F32)],
        compiler_params=_params("arbitrary", "arbitrary", "arbitrary"),
        name="fox_attention",
    )(q, k, v)


def _merge_core(x_ref, yp_ref, yc_ref, ya_ref, gates_ref, wp_ref, wc_ref, wa_ref, wo_ref, gf_ref):
    d = x_ref.shape[1]
    gates = gates_ref[...]
    merged = (gates[:, :d].astype(F32) * jnp.dot(yp_ref[...], wp_ref[...], preferred_element_type=F32)
              + gates[:, d:2 * d].astype(F32) * jnp.dot(yc_ref[...], wc_ref[...], preferred_element_type=F32)
              + gates[:, 2 * d:].astype(F32) * jnp.dot(ya_ref[...], wa_ref[...], preferred_element_type=F32))
    xn = x_ref[...] + jnp.dot(merged.astype(BF16), wo_ref[...], preferred_element_type=F32)
    ms = jnp.mean(xn * xn, axis=-1, keepdims=True)
    h2 = xn * lax.rsqrt(ms + EPS) * gf_ref[...]
    return xn, h2


def _merge_dense_kernel(x_ref, yp_ref, yc_ref, ya_ref, gates_ref, wp_ref, wc_ref, wa_ref, wo_ref, gf_ref,
                        xn_ref, h2_ref):
    xn, h2 = _merge_core(x_ref, yp_ref, yc_ref, ya_ref, gates_ref, wp_ref, wc_ref, wa_ref, wo_ref, gf_ref)
    xn_ref[...] = xn
    h2_ref[...] = h2.astype(BF16)


def _merge_moe_kernel(x_ref, yp_ref, yc_ref, ya_ref, gates_ref, wp_ref, wc_ref, wa_ref, wo_ref, gf_ref,
                      wr_ref, tri_ref, xn_ref, h2t_ref, route_ref, counts_ref, carry_ref):
    tm, d = x_ref.shape

    @pl.when(pl.program_id(0) == 0)
    def _():
        carry_ref[...] = jnp.zeros_like(carry_ref)

    xn, h2 = _merge_core(x_ref, yp_ref, yc_ref, ya_ref, gates_ref, wp_ref, wc_ref, wa_ref, wo_ref, gf_ref)
    xn_ref[...] = xn
    for c in range(d // LANES):
        h2t_ref[pl.ds(c, tm, stride=SUBLANES), :] = h2[:, c * LANES:(c + 1) * LANES]

    h_hi = h2.astype(BF16)
    h_lo = (h2 - h_hi.astype(F32)).astype(BF16)
    both = jnp.dot(h_hi, wr_ref[...], preferred_element_type=F32)
    logits = (both[:, :LANES] + both[:, LANES:]
              + jnp.dot(h_lo, wr_ref[:, :LANES], preferred_element_type=F32))
    lane = lax.broadcasted_iota(jnp.int32, (tm, LANES), 1)
    lanef = lane.astype(F32)
    lg = jnp.where(lane < N_EXPERTS, logits, -jnp.inf)
    m1 = jnp.max(lg, axis=-1, keepdims=True)
    i1 = jnp.min(jnp.where(lg == m1, lanef, float(LANES)), axis=-1, keepdims=True)
    lg2 = jnp.where(lanef == i1, -jnp.inf, lg)
    m2 = jnp.max(lg2, axis=-1, keepdims=True)
    i2 = jnp.min(jnp.where(lg2 == m2, lanef, float(LANES)), axis=-1, keepdims=True)
    e2 = jnp.exp(m2 - m1)
    w1 = 1.0 / (1.0 + e2)
    w2 = e2 / (1.0 + e2)

    is1 = lanef == i1
    is2 = lanef == i2
    sel = jnp.where(is1 | is2, 1.0, 0.0)
    rank = jnp.dot(tri_ref[...], sel.astype(BF16), preferred_element_type=F32) + carry_ref[0:1, :]
    r1 = jnp.sum(jnp.where(is1, rank, 0.0), axis=-1, keepdims=True)
    r2 = jnp.sum(jnp.where(is2, rank, 0.0), axis=-1, keepdims=True)
    total = rank[tm - 1:tm, :] + sel[tm - 1:tm, :]
    carry_ref[0:1, :] = total
    counts_ref[...] = jnp.broadcast_to(total, counts_ref.shape)

    route = jnp.zeros((tm, LANES), F32)
    for li, val in enumerate((i1, i2, w1, w2, r1, r2)):
        route = jnp.where(lane == li, val, route)
    route_ref[...] = route


def _merge(x2, yp, yc, ya, gates, wp, wc, wa, wo, gf, wr=None, tri=None, *, tm):
    n, d = x2.shape
    row = lambda w: pl.BlockSpec((tm, w), lambda i: (i, 0))
    ins = [x2, yp, yc, ya, gates, wp, wc, wa, wo, gf]
    in_specs = [row(d), row(POOL_WIDTH), row(CONV_WIDTH), row(ya.shape[1]), row(N_BRANCH * d)] + [
        _const_spec(a.shape) for a in (wp, wc, wa, wo, gf)]
    if wr is None:
        return pl.pallas_call(
            _merge_dense_kernel,
            grid=(n // tm,),
            in_specs=in_specs,
            out_specs=(row(d), row(d)),
            out_shape=(jax.ShapeDtypeStruct((n, d), F32), jax.ShapeDtypeStruct((n, d), BF16)),
            compiler_params=_params("arbitrary"),
            name="merge_dense",
        )(*ins)
    return pl.pallas_call(
        _merge_moe_kernel,
        grid=(n // tm,),
        in_specs=in_specs + [_const_spec(wr.shape), _const_spec(tri.shape)],
        out_specs=(row(d), pl.BlockSpec((tm * SUBLANES, LANES), lambda i: (i, 0)), row(LANES),
                   pl.BlockSpec((SUBLANES, LANES), lambda i: (0, 0))),
        out_shape=(jax.ShapeDtypeStruct((n, d), F32), jax.ShapeDtypeStruct((n * SUBLANES, LANES), F32),
                   jax.ShapeDtypeStruct((n, LANES), F32), jax.ShapeDtypeStruct((SUBLANES, LANES), F32)),
        scratch_shapes=[pltpu.VMEM((SUBLANES, LANES), F32)],
        compiler_params=_params("arbitrary"),
        name="merge_moe",
    )(*ins, wr, tri)


FF_SUB = 256


def _swiglu_accumulate(xb, w1_ref, w3_ref, w2_ref, acc_ref):
    fc = w1_ref.shape[2]
    for start in range(0, fc, FF_SUB):
        cols = slice(start, min(start + FF_SUB, fc))
        a = jnp.dot(xb, w1_ref[0, :, cols], preferred_element_type=F32)
        b = jnp.dot(xb, w3_ref[0, :, cols], preferred_element_type=F32)
        hmid = (a * jax.nn.sigmoid(a) * b).astype(BF16)
        acc_ref[...] += jnp.dot(hmid, w2_ref[0, cols, :], preferred_element_type=F32)


def _ffn_dense_kernel(x_ref, h_ref, w1_ref, w3_ref, w2_ref, o_ref, acc_ref):
    f = pl.program_id(1)

    @pl.when(f == 0)
    def _():
        acc_ref[...] = x_ref[...]

    _swiglu_accumulate(h_ref[...], w1_ref, w3_ref, w2_ref, acc_ref)

    @pl.when(f == pl.num_programs(1) - 1)
    def _():
        o_ref[...] = acc_ref[...]


def _ffn_dense(x2, h2, w1, w3, w2, *, tm, fc):
    n, d = x2.shape
    ff = w1.shape[2]
    row = lambda: pl.BlockSpec((tm, d), lambda i, f: (i, 0))
    once = dict(pipeline_mode=pl.Buffered(1)) if fc == ff else {}
    return pl.pallas_call(
        _ffn_dense_kernel,
        grid=(n // tm, ff // fc),
        in_specs=[row(), row(),
                  pl.BlockSpec((1, d, fc), lambda i, f: (0, 0, f), **once),
                  pl.BlockSpec((1, d, fc), lambda i, f: (0, 0, f), **once),
                  pl.BlockSpec((1, fc, d), lambda i, f: (0, f, 0), **once)],
        out_specs=row(),
        out_shape=jax.ShapeDtypeStruct((n, d), F32),
        scratch_shapes=[pltpu.VMEM((tm, d), F32)],
        compiler_params=_params("arbitrary", "arbitrary"),
        name="ffn_dense",
    )(x2, h2, w1, w3, w2)


def _ffn_grouped_kernel(te_ref, nused_ref, xs_ref, w1_ref, w3_ref, w2_ref, y_ref, xb_ref, acc_ref):
    t, d = xb_ref.shape
    i = pl.program_id(0)
    f = pl.program_id(1)
    live = i < nused_ref[0]

    @pl.when(f == 0)
    def _():
        for c in range(d // LANES):
            xb_ref[:, c * LANES:(c + 1) * LANES] = xs_ref[pl.ds(c, t, stride=SUBLANES), :].astype(BF16)
        acc_ref[...] = jnp.zeros_like(acc_ref)

    @pl.when(live)
    def _():
        _swiglu_accumulate(xb_ref[...], w1_ref, w3_ref, w2_ref, acc_ref)

    @pl.when(f == pl.num_programs(1) - 1)
    def _():
        for c in range(d // LANES):
            y_ref[pl.ds(c, t, stride=SUBLANES), :] = acc_ref[:, c * LANES:(c + 1) * LANES]


def _ffn_grouped(tile_expert, n_used, xs, w1, w3, w2, *, t, fc):
    rows, _ = xs.shape
    _, d, ff = w1.shape
    n_tiles = rows // (t * SUBLANES)
    blk = pl.BlockSpec((t * SUBLANES, LANES), lambda i, f, te, nu: (i, 0))
    grid_spec = pltpu.PrefetchScalarGridSpec(
        num_scalar_prefetch=2,
        grid=(n_tiles, ff // fc),
        in_specs=[blk,
                  pl.BlockSpec((1, d, fc), lambda i, f, te, nu: (te[i], 0, f)),
                  pl.BlockSpec((1, d, fc), lambda i, f, te, nu: (te[i], 0, f)),
                  pl.BlockSpec((1, fc, d), lambda i, f, te, nu: (te[i], f, 0))],
        out_specs=blk,
        scratch_shapes=[pltpu.VMEM((t, d), BF16), pltpu.VMEM((t, d), F32)],
    )
    return pl.pallas_call(
        _ffn_grouped_kernel,
        grid_spec=grid_spec,
        out_shape=jax.ShapeDtypeStruct(xs.shape, F32),
        compiler_params=_params("arbitrary", "arbitrary"),
        name="ffn_grouped",
    )(tile_expert, n_used, xs, w1, w3, w2)


def _dispatch_kernel(pos_ref, h_ref, xs_in, xs_hbm, sem):
    del xs_in
    td = pos_ref.shape[2] // 2

    def row_copy(tt, kk):
        slot = pos_ref[0, 0, 2 * tt + kk]
        return pltpu.make_async_copy(h_ref.at[pl.ds(tt * SUBLANES, SUBLANES)],
                                     xs_hbm.at[pl.ds(slot * SUBLANES, SUBLANES)], sem)

    def issue(tt, carry):
        for kk in range(2):
            row_copy(tt, kk).start()
        return carry

    lax.fori_loop(0, td, issue, 0, unroll=4)
    for _ in range(2):
        pltpu.make_async_copy(h_ref, xs_hbm.at[pl.ds(0, td * SUBLANES)], sem).wait()


def _dispatch(pos2, h2t, n_slots, *, td):
    n = h2t.shape[0] // SUBLANES
    xs0 = jnp.zeros((n_slots * SUBLANES, LANES), F32)
    return pl.pallas_call(
        _dispatch_kernel,
        grid=(n // td,),
        in_specs=[pl.BlockSpec((1, 1, 2 * td), lambda i: (i, 0, 0), memory_space=pltpu.SMEM),
                  pl.BlockSpec((td * SUBLANES, LANES), lambda i: (i, 0)),
                  pl.BlockSpec(memory_space=pl.ANY)],
        out_specs=pl.BlockSpec(memory_space=pl.ANY),
        out_shape=jax.ShapeDtypeStruct(xs0.shape, F32),
        scratch_shapes=[pltpu.SemaphoreType.DMA(())],
        input_output_aliases={2: 0},
        compiler_params=_params("arbitrary"),
        name="moe_dispatch",
    )(pos2, h2t, xs0)


def _combine_kernel(pos_ref, posn_ref, x_ref, route_ref, y_hbm, o_ref, buf, sem):
    tc, d = x_ref.shape
    i = pl.program_id(0)
    cur = i % 2

    def row_copy(p_ref, buf_slot, tt, kk):
        slot = p_ref[0, 0, 2 * tt + kk]
        return pltpu.make_async_copy(y_hbm.at[pl.ds(slot * SUBLANES, SUBLANES)],
                                     buf.at[buf_slot, kk, pl.ds(tt * SUBLANES, SUBLANES)], sem.at[buf_slot])

    def issue_all(p_ref, buf_slot):
        def issue(tt, carry):
            for kk in range(2):
                row_copy(p_ref, buf_slot, tt, kk).start()
            return carry

        lax.fori_loop(0, tc, issue, 0, unroll=4)

    @pl.when(i == 0)
    def _():
        issue_all(pos_ref, 0)

    @pl.when(i + 1 < pl.num_programs(0))
    def _():
        issue_all(posn_ref, 1 - cur)

    for kk in range(2):
        pltpu.make_async_copy(y_hbm.at[pl.ds(0, tc * SUBLANES)], buf.at[cur, kk], sem.at[cur]).wait()

    route = route_ref[...]
    w1 = route[:, 2:3]
    w2 = route[:, 3:4]
    for c in range(d // LANES):
        y1 = buf[cur, 0, pl.ds(c, tc, stride=SUBLANES), :]
        y2 = buf[cur, 1, pl.ds(c, tc, stride=SUBLANES), :]
        o_ref[:, c * LANES:(c + 1) * LANES] = x_ref[:, c * LANES:(c + 1) * LANES] + (w1 * y1 + w2 * y2)


def _combine(pos2, x2, route, y, *, tc):
    n, d = x2.shape
    last = n // tc - 1
    return pl.pallas_call(
        _combine_kernel,
        grid=(n // tc,),
        in_specs=[pl.BlockSpec((1, 1, 2 * tc), lambda i: (i, 0, 0), memory_space=pltpu.SMEM),
                  pl.BlockSpec((1, 1, 2 * tc), lambda i: (jnp.minimum(i + 1, last), 0, 0), memory_space=pltpu.SMEM),
                  pl.BlockSpec((tc, d), lambda i: (i, 0)),
                  pl.BlockSpec((tc, LANES), lambda i: (i, 0)),
                  pl.BlockSpec(memory_space=pl.ANY)],
        out_specs=pl.BlockSpec((tc, d), lambda i: (i, 0)),
        out_shape=jax.ShapeDtypeStruct((n, d), F32),
        scratch_shapes=[pltpu.VMEM((2, 2, tc * SUBLANES, LANES), F32), pltpu.SemaphoreType.DMA((2,))],
        compiler_params=_params("arbitrary"),
        name="moe_combine",
    )(pos2, pos2, x2, route, y)


def _tiles(b, s):
    n = b * s
    pick = lambda full, unit: full if unit % full == 0 else unit
    return dict(
        tm_in=pick(512, s), tq=pick(1024, s), tm_merge=pick(512, n),
        tm_ffn=pick(1024, n), fc_dense=2816, t_moe=pick(1024, n) if n >= 8192 else 256, fc_moe=1792,
        td=pick(512, n), tc=pick(256, n),
    )


def _head_layout_constants():
    pq = np.zeros((ATTN_WIDTH + LANES, ATTN_HEADS * LANES), np.float32)
    pk = np.zeros_like(pq)
    oq = np.zeros((1, ATTN_HEADS * LANES), np.float32)
    ok = np.zeros_like(oq)
    ov = np.zeros_like(oq)
    for h in range(ATTN_HEADS):
        base = h * LANES
        for j in range(HEAD_DIM):
            pq[h * HEAD_DIM + j, base + j] = 1.0
            pk[h * HEAD_DIM + j, base + j] = 1.0
        for part in range(3):
            pq[ATTN_WIDTH + part * ATTN_HEADS + h, base + HEAD_DIM + part] = 1.0
            pk[ATTN_WIDTH + part * ATTN_HEADS + h, base + HEAD_DIM + 3 + part] = -1.0
            oq[0, base + HEAD_DIM + 3 + part] = 1.0
            ok[0, base + HEAD_DIM + part] = 1.0
        ov[0, base + HEAD_DIM:base + LANES] = 1.0
    return (jnp.asarray(pq, BF16), jnp.asarray(pk, BF16), jnp.asarray(oq), jnp.asarray(ok), jnp.asarray(ov))


def _block_diag(blocks):
    g, r, c = blocks.shape
    eye = jnp.eye(g, dtype=blocks.dtype)
    return (eye[:, None, :, None] * blocks[:, :, None, :]).reshape(g * r, g * c)


def _forward(x, g_mix, w_in, b_gate, b_forget, pool_w, pool_scale, conv_w, conv_b, conv_ln_g, conv_ln_b,
             q_norm_g, k_norm_g, w_br_pool, w_br_conv, w_br_attn, w_out, g_ffn, ffn_w1, ffn_w3, ffn_w2,
             router_w, exp_w1, exp_w3, exp_w2, tiles):
    b, s, d = x.shape
    n = b * s
    depth = w_in.shape[0]
    t = tiles

    off_conv = POOL_WIDTH
    off_q = off_conv + 2 * CONV_WIDTH
    off_f = off_q + 3 * ATTN_WIDTH
    off_g = off_f + ATTN_HEADS

    wlane = jnp.repeat(jnp.asarray(POOL_WINDOWS, F32), POOL_GROUP_DIM)[None, :]
    bd = _block_diag(jnp.full((ATTN_HEADS, HEAD_DIM, HEAD_DIM), 1.0 / HEAD_DIM, F32)).astype(BF16)
    tri = (lax.broadcasted_iota(jnp.int32, (t["tm_merge"], t["tm_merge"]), 1)
           < lax.broadcasted_iota(jnp.int32, (t["tm_merge"], t["tm_merge"]), 0)).astype(BF16)
    ffn_w1b, ffn_w3b, ffn_w2b = ffn_w1.astype(BF16), ffn_w3.astype(BF16), ffn_w2.astype(BF16)
    exp_w1b, exp_w3b, exp_w2b = exp_w1.astype(BF16), exp_w3.astype(BF16), exp_w2.astype(BF16)

    t_moe = t["t_moe"]
    n_tiles = (2 * n) // t_moe + N_EXPERTS
    n_slots = n_tiles * t_moe

    pq, pk, oq, ok, ov = _head_layout_constants()

    def head_pad(w, axis):
        shp = w.shape
        w = w.reshape(shp[:axis] + (ATTN_HEADS, HEAD_DIM) + shp[axis + 1:])
        pad = [(0, 0)] * w.ndim
        pad[axis + 1] = (0, LANES - HEAD_DIM)
        return jnp.pad(w, pad).reshape(shp[:axis] + (ATTN_HEADS * LANES,) + shp[axis + 1:])

    for l in range(depth):
        wl = w_in[l]
        wloc = wl[:, :off_q].astype(BF16)
        wqk = wl[:, off_q:off_q + 2 * ATTN_WIDTH].astype(BF16)
        wv = head_pad(wl[:, off_q + 2 * ATTN_WIDTH:off_f], 1).astype(BF16)
        wf3 = jnp.pad(jnp.tile(wl[:, off_f:off_g], (1, 3)), ((0, 0), (0, LANES - 3 * ATTN_HEADS))).astype(BF16)
        bf3 = jnp.pad(jnp.tile(b_forget[l], 3), (0, LANES - 3 * ATTN_HEADS))[None, :]
        wg = wl[:, off_g:].astype(BF16)
        qg = jnp.tile(q_norm_g[l], ATTN_HEADS)[None, :] * (HEAD_DIM ** -0.5 * LOG2E)
        kg = jnp.tile(k_norm_g[l], ATTN_HEADS)[None, :]
        yp, yc, q, k, v, gates = _inproj(
            x, g_mix[l][None, :], wloc, wqk, wv, wf3, wg, b_gate[l][None, :], bf3, qg, kg, bd, pq, pk, oq, ok, ov,
            _block_diag(pool_w[l]).astype(BF16), pool_scale[l][None, :], wlane,
            conv_w[l], conv_b[l][None, :], conv_ln_g[l][None, :], conv_ln_b[l][None, :],
            tm=t["tm_in"])

        ya = _attention(q, k, v, tq=t["tq"], nh=2)

        merge_args = (x.reshape(n, d), yp.reshape(n, -1), yc.reshape(n, -1), ya.reshape(n, -1),
                      gates.reshape(n, -1), w_br_pool[l].astype(BF16), w_br_conv[l].astype(BF16),
                      head_pad(w_br_attn[l], 0).astype(BF16), w_out[l].astype(BF16), g_ffn[l][None, :])
        i = l // 2
        if l % 2 == 0:
            xn, h2 = _merge(*merge_args, tm=t["tm_merge"])
            x2 = _ffn_dense(xn, h2, ffn_w1b[i:i + 1], ffn_w3b[i:i + 1], ffn_w2b[i:i + 1],
                            tm=t["tm_ffn"], fc=t["fc_dense"])
        else:
            wr32 = jnp.pad(router_w[i], ((0, 0), (0, LANES - N_EXPERTS)))
            wr_hi = wr32.astype(BF16)
            wr = jnp.concatenate([wr_hi, (wr32 - wr_hi.astype(F32)).astype(BF16)], axis=1)
            xn, h2t, route, counts = _merge(*merge_args, wr, tri, tm=t["tm_merge"])
            cnt = counts[0, :N_EXPERTS].astype(jnp.int32)
            ntile_e = (cnt + t_moe - 1) // t_moe
            ends = jnp.cumsum(ntile_e) * t_moe
            starts = ends - ntile_e * t_moe
            idx = route[:, 0:2].astype(jnp.int32)
            rank = route[:, 4:6].astype(jnp.int32)
            pos = starts[idx] + rank
            tile_first = jnp.arange(n_tiles, dtype=jnp.int32) * t_moe
            tile_expert = jnp.minimum(jnp.sum(tile_first[:, None] >= ends[None, :], axis=1),
                                      N_EXPERTS - 1).astype(jnp.int32)
            n_used = (ends[-1:] // t_moe).astype(jnp.int32)
            xs = _dispatch(pos.reshape(n // t["td"], 1, 2 * t["td"]), h2t, n_slots, td=t["td"])
            y = _ffn_grouped(tile_expert, n_used, xs, exp_w1b[i], exp_w3b[i], exp_w2b[i], t=t_moe, fc=t["fc_moe"])
            x2 = _combine(pos.reshape(n // t["tc"], 1, 2 * t["tc"]), xn, route, y, tc=t["tc"])
        x = x2.reshape(b, s, d)
    return x


def kernel(x, g_mix, w_in, b_gate, b_forget, pool_w, pool_scale, conv_w, conv_b, conv_ln_g, conv_ln_b,
           q_norm_g, k_norm_g, w_br_pool, w_br_conv, w_br_attn, w_out, g_ffn, ffn_w1, ffn_w3, ffn_w2,
           router_w, exp_w1, exp_w3, exp_w2):
    return _forward(x, g_mix, w_in, b_gate, b_forget, pool_w, pool_scale, conv_w, conv_b, conv_ln_g, conv_ln_b,
                    q_norm_g, k_norm_g, w_br_pool, w_br_conv, w_br_attn, w_out, g_ffn, ffn_w1, ffn_w3, ffn_w2,
                    router_w, exp_w1, exp_w3, exp_w2, _tiles(x.shape[0], x.shape[1]))
```
